```python
import math
import jax
import jax.numpy as jnp
from jax import lax
import numpy as np

D_MODEL = 1024
BATCH = 4
SEQ = 4096
DEPTH = 4
DEC_BATCH = 128
DEC_SEQ = 8
PAST_LEN = 2048
PAGE_SIZE = 128

N_AB = (DEPTH + 1) // 2
N_C = DEPTH // 2
HEAD_DIM = 64
MOBA_WIDTH = D_MODEL // 2
MOBA_HEADS = MOBA_WIDTH // HEAD_DIM
MOBA_BLOCK = 256
MOBA_TOPK = 3
MOBA_Q_CHUNK = 32
MOBA_ROWS = 128
S5_WIDTH = D_MODEL - MOBA_WIDTH
S5_GROUP = 16
S5_GROUPS = S5_WIDTH // S5_GROUP
S5_STATE = 64
DIFF_HEADS = D_MODEL // (2 * HEAD_DIM)
DIFF_WIDTH = DIFF_HEADS * 2 * HEAD_DIM
DIFF_Q_CHUNK = 128
REL_BUCKETS = 32
REL_MAX_DIST = 128
D_FF = 2816
CONV_WIDTH = 3
RMS_EPS = 1e-6
F32 = jnp.float32

kernel_name = 'hybrid_moba_s5_diffattn_convffn_step'


def rmsnorm(x, g):
    x32 = x.astype(F32)
    y = x32 * lax.rsqrt(jnp.mean(x32 * x32, axis=-1, keepdims=True) + RMS_EPS)
    return (y * g.astype(F32)).astype(x.dtype)


def rel_bucket(dist):
    dist = jnp.maximum(dist, 0)
    exact = REL_BUCKETS // 2
    log_ratio = jnp.log(jnp.maximum(dist, 1).astype(F32) / exact) / math.log(REL_MAX_DIST / exact)
    large = exact + (log_ratio * (REL_BUCKETS - exact)).astype(jnp.int32)
    return jnp.where(dist < exact, dist, jnp.minimum(large, REL_BUCKETS - 1))


def moba_core(q, kb, vb, kmean, q_pos, blk, rel_bias):
    b, H, Q, _ = q.shape
    nb = kb.shape[2]
    n_sel = min(MOBA_TOPK, nb)
    gate = jnp.einsum('bhqd,bhnd->bhqn', q.astype(F32), kmean)
    gate = jnp.where(jnp.arange(nb) < blk, gate, -jnp.inf)
    _, sel = lax.top_k(gate, n_sel)
    sel_valid = jnp.arange(n_sel) < blk
    bi = jnp.arange(b)[:, None, None, None]
    hi = jnp.arange(H)[None, :, None, None]
    k_sel = kb[bi, hi, sel]
    v_sel = vb[bi, hi, sel]
    k_own = lax.dynamic_index_in_dim(kb, blk, axis=2, keepdims=False)
    v_own = lax.dynamic_index_in_dim(vb, blk, axis=2, keepdims=False)
    offs = jnp.arange(MOBA_BLOCK)
    pos_sel = sel[..., None] * MOBA_BLOCK + offs
    pos_own = blk * MOBA_BLOCK + offs
    table = rel_bias.astype(F32).T
    bias_sel = table[hi[..., None], rel_bucket(q_pos[:, None, None] - pos_sel)]
    bias_own = table[:, rel_bucket(q_pos[:, None] - pos_own[None, :])]
    s_sel = jnp.einsum('bhqd,bhqjsd->bhqjs', q, k_sel, preferred_element_type=F32) + bias_sel
    s_sel = jnp.where(sel_valid[:, None], s_sel, -jnp.inf)
    s_own = jnp.einsum('bhqd,bhsd->bhqs', q, k_own, preferred_element_type=F32) + bias_own
    s_own = jnp.where(pos_own[None, :] <= q_pos[:, None], s_own, -jnp.inf)
    n_k = n_sel * MOBA_BLOCK
    p = jax.nn.softmax(jnp.concatenate([s_sel.reshape(b, H, Q, n_k), s_own], axis=-1), axis=-1)
    p_sel = p[..., :n_k].reshape(b, H, Q, n_sel, MOBA_BLOCK)
    p_own = p[..., n_k:]
    return (jnp.einsum('bhqjs,bhqjsd->bhqd', p_sel, v_sel, preferred_element_type=F32)
            + jnp.einsum('bhqs,bhsd->bhqd', p_own, v_own, preferred_element_type=F32))


def moba_attention(q, k, v, q_start, rel_bias):
    B, Tq, H, Dh = q.shape
    L = k.shape[1]
    nb = -(-L // MOBA_BLOCK)
    pad = ((0, 0), (0, nb * MOBA_BLOCK - L), (0, 0), (0, 0))
    kb = jnp.pad(k, pad).reshape(B, nb, MOBA_BLOCK, H, Dh).transpose(0, 3, 1, 2, 4)
    vb = jnp.pad(v, pad).reshape(B, nb, MOBA_BLOCK, H, Dh).transpose(0, 3, 1, 2, 4)
    kmean = jnp.mean(kb.astype(F32), axis=3)
    qh = q.transpose(0, 2, 1, 3)
    qc = math.gcd(Tq, MOBA_Q_CHUNK)
    nqc = Tq // qc
    bc = math.gcd(B, max(1, MOBA_ROWS // qc))
    nbc = B // bc

    def step(i):
        b0 = (i // nqc) * bc
        t0 = (i % nqc) * qc
        qs = lax.dynamic_slice_in_dim(lax.dynamic_slice_in_dim(qh, b0, bc, 0), t0, qc, 2)
        q_pos = q_start + t0 + jnp.arange(qc)
        blk = (q_start + t0) // MOBA_BLOCK
        return moba_core(qs, lax.dynamic_slice_in_dim(kb, b0, bc, 0),
                         lax.dynamic_slice_in_dim(vb, b0, bc, 0),
                         lax.dynamic_slice_in_dim(kmean, b0, bc, 0), q_pos, blk, rel_bias)

    o = lax.map(step, jnp.arange(nbc * nqc))
    o = o.reshape(nbc, nqc, bc, H, qc, Dh).transpose(0, 2, 1, 4, 3, 5)
    return o.reshape(B, Tq, H * Dh)


def _complex_affine_combine(e1, e2):
    a1r, a1i, b1r, b1i = e1
    a2r, a2i, b2r, b2i = e2
    return (a2r * a1r - a2i * a1i, a2r * a1i + a2i * a1r,
            a2r * b1r - a2i * b1i + b2r, a2r * b1i + a2i * b1r + b2i)


def s5_scan(u, s0_re, s0_im, a_re, a_im, log_dt, b_re, b_im, c_re, c_im, d_skip):
    u = u.astype(F32)
    a_re, a_im = a_re.astype(F32), a_im.astype(F32)
    dt = jnp.exp(log_dt.astype(F32))[:, None]
    mag = jnp.exp(a_re * dt)
    lb_re, lb_im = mag * jnp.cos(a_im * dt), mag * jnp.sin(a_im * dt)
    den = a_re * a_re + a_im * a_im
    f_re = ((lb_re - 1.0) * a_re + lb_im * a_im) / den
    f_im = (lb_im * a_re - (lb_re - 1.0) * a_im) / den
    b_re, b_im = b_re.astype(F32), b_im.astype(F32)
    bb_re = f_re[..., None] * b_re - f_im[..., None] * b_im
    bb_im = f_re[..., None] * b_im + f_im[..., None] * b_re
    e_re = jnp.einsum('btgp,gnp->btgn', u, bb_re)
    e_im = jnp.einsum('btgp,gnp->btgn', u, bb_im)
    p_re = jnp.broadcast_to(lb_re, e_re.shape)
    p_im = jnp.broadcast_to(lb_im, e_im.shape)
    p_re, p_im, h_re, h_im = lax.associative_scan(_complex_affine_combine, (p_re, p_im, e_re, e_im), axis=1)
    s0_re, s0_im = s0_re.astype(F32)[:, None], s0_im.astype(F32)[:, None]
    s_re = h_re + p_re * s0_re - p_im * s0_im
    s_im = h_im + p_re * s0_im + p_im * s0_re
    y = (jnp.einsum('btgn,gpn->btgp', s_re, c_re.astype(F32))
         - jnp.einsum('btgn,gpn->btgp', s_im, c_im.astype(F32)) + d_skip.astype(F32) * u)
    return y, s_re[:, -1], s_im[:, -1]


def diff_attention(q, k, v, q_start, lam, rel_bias):
    B, Tq, H = q.shape[:3]
    L = k.shape[1]
    qc = math.gcd(Tq, DIFF_Q_CHUNK)
    nqc = Tq // qc
    k_pos = jnp.arange(L)
    table = rel_bias.astype(F32)

    def step(i):
        qs = lax.dynamic_slice_in_dim(q, i * qc, qc, axis=1)
        q_pos = q_start + i * qc + jnp.arange(qc)
        s = jnp.einsum('bqhcd,bkhcd->bhcqk', qs, k, preferred_element_type=F32)
        dist = q_pos[:, None] - k_pos[None, :]
        bias = jnp.moveaxis(table[rel_bucket(dist)], -1, 0)
        s = jnp.where(dist >= 0, s + bias[:, None], -jnp.inf)
        p = jax.nn.softmax(s, axis=-1)
        a = p[:, :, 0] - lam * p[:, :, 1]
        return jnp.einsum('bhqk,bkhd->bqhd', a, v, preferred_element_type=F32)

    o = lax.map(step, jnp.arange(nqc))
    return jnp.moveaxis(o, 0, 1).reshape(B, Tq, H, 2 * HEAD_DIM)


def ab_mixer(h, k_past, v_past, s0_re, s0_im, q_start, rel_bias, w_in, w_out, q_gain, k_gain,
             a_re, a_im, log_dt, b_re, b_im, c_re, c_im, d_skip, w_glu):
    B, T, _ = h.shape
    q, k, v, u = jnp.split(h @ w_in, [MOBA_WIDTH, 2 * MOBA_WIDTH, 3 * MOBA_WIDTH], axis=-1)
    q = rmsnorm(q.reshape(B, T, MOBA_HEADS, HEAD_DIM), q_gain) * (HEAD_DIM ** -0.5)
    k = rmsnorm(k.reshape(B, T, MOBA_HEADS, HEAD_DIM), k_gain)
    v = v.reshape(B, T, MOBA_HEADS, HEAD_DIM)
    k_all = jnp.concatenate([k_past.astype(k.dtype), k], axis=1)
    v_all = jnp.concatenate([v_past.astype(v.dtype), v], axis=1)
    o_a = moba_attention(q, k_all, v_all, q_start, rel_bias).astype(h.dtype)
    y, s_re, s_im = s5_scan(u.reshape(B, T, S5_GROUPS, S5_GROUP), s0_re, s0_im, a_re, a_im, log_dt,
                            b_re, b_im, c_re, c_im, d_skip)
    y = jax.nn.gelu(y.reshape(B, T, S5_WIDTH))
    o_b = (y * jax.nn.sigmoid(y @ w_glu.astype(F32))).astype(h.dtype)
    out = jnp.concatenate([o_a, o_b], axis=-1) @ w_out
    return out, k, v, s_re, s_im


def c_mixer(h, k_past, v_past, q_start, lam_init, rel_bias, w_in, w_out, q_gain, k_gain,
            lq1, lk1, lq2, lk2, head_gain):
    B, T, _ = h.shape
    q, k, v = jnp.split(h @ w_in, [DIFF_WIDTH, 2 * DIFF_WIDTH], axis=-1)
    q = rmsnorm(q.reshape(B, T, DIFF_HEADS, 2, HEAD_DIM), q_gain) * (HEAD_DIM ** -0.5)
    k = rmsnorm(k.reshape(B, T, DIFF_HEADS, 2, HEAD_DIM), k_gain)
    v = v.reshape(B, T, DIFF_HEADS, 2 * HEAD_DIM)
    k_all = jnp.concatenate([k_past.astype(k.dtype), k], axis=1)
    v_all = jnp.concatenate([v_past.astype(v.dtype), v], axis=1)
    lam = (jnp.exp(jnp.sum(lq1.astype(F32) * lk1.astype(F32)))
           - jnp.exp(jnp.sum(lq2.astype(F32) * lk2.astype(F32))) + lam_init)
    o = diff_attention(q, k_all, v_all, q_start, lam, rel_bias)
    o = rmsnorm(o, head_gain) * (1.0 - lam_init)
    return o.reshape(B, T, DIFF_WIDTH).astype(h.dtype) @ w_out, k, v


def conv_ffn(h, buf, w_up, conv_w, conv_b, w_down):
    T = h.shape[1]
    g, u = jnp.split(h @ w_up, 2, axis=-1)
    gx = jnp.concatenate([buf.astype(g.dtype), g], axis=1)
    c = conv_b
    for j in range(CONV_WIDTH):
        c = c + conv_w[j] * gx[:, j:j + T]
    return (jax.nn.silu(c) * u) @ w_down, gx[:, T:]


def setup_inputs(seed: int = 0) -> dict:
    key = jax.random.key(seed)
    keys = iter(jax.random.split(key, 64))

    def normal(shape, scale=1.0):
        return scale * jax.random.normal(next(keys), shape, F32)

    def gain(shape):
        return 1.0 + normal(shape, 0.02)

    n_pages = PAST_LEN // PAGE_SIZE
    n_phys = (DEC_BATCH * n_pages * 5) // 4
    page_table = jax.random.permutation(next(keys), n_phys)[:DEC_BATCH * n_pages]
    page_table = page_table.reshape(DEC_BATCH, n_pages).astype(jnp.int32)
    ab_in_width = 3 * MOBA_WIDTH + S5_WIDTH
    log_dt = jax.random.uniform(next(keys), (N_AB, S5_GROUPS), F32, math.log(1e-3), math.log(1e-1))
    a_im_init = math.pi * jnp.arange(S5_STATE, dtype=F32)
    return {
        'x_prompt': normal((BATCH, SEQ, D_MODEL)),
        'x_sample': normal((DEC_BATCH, DEC_SEQ, D_MODEL)),
        'cache_moba_k': normal((n_phys, N_AB, PAGE_SIZE, MOBA_HEADS, HEAD_DIM)),
        'cache_moba_v': normal((n_phys, N_AB, PAGE_SIZE, MOBA_HEADS, HEAD_DIM)),
        'state_s5_re': normal((N_AB, DEC_BATCH, S5_GROUPS, S5_STATE), 0.5),
        'state_s5_im': normal((N_AB, DEC_BATCH, S5_GROUPS, S5_STATE), 0.5),
        'cache_diff_k': normal((n_phys, N_C, PAGE_SIZE, DIFF_HEADS, 2, HEAD_DIM)),
        'cache_diff_v': normal((n_phys, N_C, PAGE_SIZE, DIFF_HEADS, 2 * HEAD_DIM)),
        'state_ffn_conv': normal((DEPTH, DEC_BATCH, CONV_WIDTH - 1, D_FF)),
        'page_table': page_table,
        'rel_bias': normal((REL_BUCKETS, MOBA_HEADS), 0.5),
        'ab_norm': gain((N_AB, D_MODEL)),
        'w_ab_in': normal((N_AB, D_MODEL, ab_in_width), D_MODEL ** -0.5),
        'w_ab_out': normal((N_AB, MOBA_WIDTH + S5_WIDTH, D_MODEL), (MOBA_WIDTH + S5_WIDTH) ** -0.5),
        'moba_q_gain': gain((N_AB, HEAD_DIM)),
        'moba_k_gain': gain((N_AB, HEAD_DIM)),
        's5_a_re': -0.5 + normal((N_AB, S5_GROUPS, S5_STATE), 0.01),
        's5_a_im': a_im_init + normal((N_AB, S5_GROUPS, S5_STATE), 0.01),
        's5_log_dt': log_dt,
        's5_b_re': normal((N_AB, S5_GROUPS, S5_STATE, S5_GROUP), (2 * S5_GROUP) ** -0.5),
        's5_b_im': normal((N_AB, S5_GROUPS, S5_STATE, S5_GROUP), (2 * S5_GROUP) ** -0.5),
        's5_c_re': normal((N_AB, S5_GROUPS, S5_GROUP, S5_STATE), S5_STATE ** -0.5),
        's5_c_im': normal((N_AB, S5_GROUPS, S5_GROUP, S5_STATE), S5_STATE ** -0.5),
        's5_d': normal((N_AB, S5_GROUPS, S5_GROUP)),
        's5_w_glu': normal((N_AB, S5_WIDTH, S5_WIDTH), S5_WIDTH ** -0.5),
        'c_norm': gain((N_C, D_MODEL)),
        'w_c_in': normal((N_C, D_MODEL, 3 * DIFF_WIDTH), D_MODEL ** -0.5),
        'w_c_out': normal((N_C, DIFF_WIDTH, D_MODEL), DIFF_WIDTH ** -0.5),
        'diff_q_gain': gain((N_C, HEAD_DIM)),
        'diff_k_gain': gain((N_C, HEAD_DIM)),
        'diff_lq1': normal((N_C, HEAD_DIM), 0.1),
        'diff_lk1': normal((N_C, HEAD_DIM), 0.1),
        'diff_lq2': normal((N_C, HEAD_DIM), 0.1),
        'diff_lk2': normal((N_C, HEAD_DIM), 0.1),
        'diff_head_gain': gain((N_C, 2 * HEAD_DIM)),
        'ffn_norm': gain((DEPTH, D_MODEL)),
        'w_ffn_up': normal((DEPTH, D_MODEL, 2 * D_FF), D_MODEL ** -0.5),
        'ffn_conv_w': normal((DEPTH, CONV_WIDTH, D_FF), CONV_WIDTH ** -0.5),
        'ffn_conv_b': normal((DEPTH, D_FF), 0.02),
        'w_ffn_down': normal((DEPTH, D_FF, D_MODEL), D_FF ** -0.5),
    }


def reference(x_prompt, x_sample, cache_moba_k, cache_moba_v, state_s5_re, state_s5_im,
              cache_diff_k, cache_diff_v, state_ffn_conv, page_table, rel_bias,
              ab_norm, w_ab_in, w_ab_out, moba_q_gain, moba_k_gain,
              s5_a_re, s5_a_im, s5_log_dt, s5_b_re, s5_b_im, s5_c_re, s5_c_im, s5_d, s5_w_glu,
              c_norm, w_c_in, w_c_out, diff_q_gain, diff_k_gain,
              diff_lq1, diff_lk1, diff_lq2, diff_lk2, diff_head_gain,
              ffn_norm, w_ffn_up, ffn_conv_w, ffn_conv_b, w_ffn_down):
    n_pages = PAST_LEN // PAGE_SIZE

    def gather_past(pool, li):
        rows = pool[page_table, li]
        return rows.reshape((rows.shape[0], n_pages * PAGE_SIZE) + rows.shape[3:])

    xp, xs = x_prompt, x_sample
    batch_p = xp.shape[0]
    mk_p, mk_s, mv_p, mv_s = [], [], [], []
    sr_p, sr_s, si_p, si_s = [], [], [], []
    dk_p, dk_s, dv_p, dv_s = [], [], [], []
    cb_p, cb_s = [], []
    for layer in range(DEPTH):
        li = layer // 2
        if layer % 2 == 0:
            shared = (rel_bias, w_ab_in[li], w_ab_out[li], moba_q_gain[li], moba_k_gain[li],
                      s5_a_re[li], s5_a_im[li], s5_log_dt[li], s5_b_re[li], s5_b_im[li],
                      s5_c_re[li], s5_c_im[li], s5_d[li], s5_w_glu[li])
            empty = jnp.zeros((batch_p, 0, MOBA_HEADS, HEAD_DIM), xp.dtype)
            zero_state = jnp.zeros((batch_p, S5_GROUPS, S5_STATE), F32)
            op, kp, vp, rp, ip = ab_mixer(rmsnorm(xp, ab_norm[li]), empty, empty, zero_state, zero_state, 0, *shared)
            os_, ks, vs, rs, is_ = ab_mixer(rmsnorm(xs, ab_norm[li]), gather_past(cache_moba_k, li),
                                            gather_past(cache_moba_v, li), state_s5_re[li], state_s5_im[li],
                                            PAST_LEN, *shared)
            mk_p.append(kp); mk_s.append(ks); mv_p.append(vp); mv_s.append(vs)
            sr_p.append(rp); sr_s.append(rs); si_p.append(ip); si_s.append(is_)
        else:
            lam_init = 0.8 - 0.6 * math.exp(-0.3 * layer)
            shared = (lam_init, rel_bias, w_c_in[li], w_c_out[li], diff_q_gain[li], diff_k_gain[li],
                      diff_lq1[li], diff_lk1[li], diff_lq2[li], diff_lk2[li], diff_head_gain[li])
            empty_k = jnp.zeros((batch_p, 0, DIFF_HEADS, 2, HEAD_DIM), xp.dtype)
            empty_v = jnp.zeros((batch_p, 0, DIFF_HEADS, 2 * HEAD_DIM), xp.dtype)
            op, kp, vp = c_mixer(rmsnorm(xp, c_norm[li]), empty_k, empty_v, 0, *shared)
            os_, ks, vs = c_mixer(rmsnorm(xs, c_norm[li]), gather_past(cache_diff_k, li),
                                  gather_past(cache_diff_v, li), PAST_LEN, *shared)
            dk_p.append(kp); dk_s.append(ks); dv_p.append(vp); dv_s.append(vs)
        xp = xp + op
        xs = xs + os_
        ffn_shared = (w_ffn_up[layer], ffn_conv_w[layer], ffn_conv_b[layer], w_ffn_down[layer])
        fp, bufp = conv_ffn(rmsnorm(xp, ffn_norm[layer]),
                            jnp.zeros((batch_p, CONV_WIDTH - 1, D_FF), xp.dtype), *ffn_shared)
        fs, bufs = conv_ffn(rmsnorm(xs, ffn_norm[layer]), state_ffn_conv[layer], *ffn_shared)
        cb_p.append(bufp); cb_s.append(bufs)
        xp = xp + fp
        xs = xs + fs

    y_prompt, y_sample = xp, xs
    moba_k_prompt, moba_k_sample = jnp.stack(mk_p, axis=1), jnp.stack(mk_s, axis=1)
    moba_v_prompt, moba_v_sample = jnp.stack(mv_p, axis=1), jnp.stack(mv_s, axis=1)
    s5_re_prompt, s5_re_sample = jnp.stack(sr_p, axis=0), jnp.stack(sr_s, axis=0)
    s5_im_prompt, s5_im_sample = jnp.stack(si_p, axis=0), jnp.stack(si_s, axis=0)
    diff_k_prompt, diff_k_sample = jnp.stack(dk_p, axis=1), jnp.stack(dk_s, axis=1)
    diff_v_prompt, diff_v_sample = jnp.stack(dv_p, axis=1), jnp.stack(dv_s, axis=1)
    ffn_conv_prompt, ffn_conv_sample = jnp.stack(cb_p, axis=0), jnp.stack(cb_s, axis=0)
    return (y_prompt, y_sample, moba_k_prompt, moba_k_sample, moba_v_prompt, moba_v_sample,
            s5_re_prompt, s5_re_sample, s5_im_prompt, s5_im_sample,
            diff_k_prompt, diff_k_sample, diff_v_prompt, diff_v_sample,
            ffn_conv_prompt, ffn_conv_sample)
```

```python
import functools
import math

import jax
import jax.numpy as jnp
from jax import lax
from jax.experimental import pallas as pl
from jax.experimental.pallas import tpu as pltpu

F32 = jnp.float32
BF16 = jnp.bfloat16

D_MODEL = 1024
HEAD_DIM = 64
MOBA_WIDTH = 512
MOBA_BLOCK = 256
MOBA_TOPK = 3
S5_WIDTH = 512
S5_GROUP = 16
S5_GROUPS = 32
S5_STATE = 64
DIFF_HEADS = 8
REL_BUCKETS = 32
REL_MAX_DIST = 128
D_FF = 2816
RMS_EPS = 1e-6
PAGE_SIZE = 128

ATT_TILE = 256
FF_CHUNK = 256
N_FF_CHUNKS = D_FF // FF_CHUNK
M_INIT = -1e30
VMEM_LIMIT = 56 * 1024 * 1024

_NT = (((1,), (1,)), ((), ()))


def _log2(n):
    assert n & (n - 1) == 0
    return n.bit_length() - 1


def _cparams(sem, vmem=VMEM_LIMIT):
    return pltpu.CompilerParams(dimension_semantics=sem, vmem_limit_bytes=vmem)


def _split(a):
    hi = a.astype(BF16)
    lo = (a - hi.astype(F32)).astype(BF16)
    return hi, lo


def _dot3(a, b):
    ah, al = _split(a)
    bh, bl = _split(b)
    d = functools.partial(jnp.dot, preferred_element_type=F32)
    return d(ah, bh) + d(ah, bl) + d(al, bh)


def _dot3_nt(a, b):
    ah, al = _split(a)
    bh, bl = _split(b)
    d = functools.partial(lax.dot_general, dimension_numbers=_NT, preferred_element_type=F32)
    return d(ah, bh) + d(ah, bl) + d(al, bh)


def _rms_rows(x, g):
    ms = jnp.mean(x * x, axis=-1, keepdims=True)
    return x * lax.rsqrt(ms + RMS_EPS) * g


def _group_mean_sq(t, gmat):
    t2 = t * t
    hi, lo = _split(t2)
    return (jnp.dot(hi, gmat, preferred_element_type=F32)
            + jnp.dot(lo, gmat, preferred_element_type=F32))


def _head_norm(t, gmat, gain):
    width = t.shape[-1]
    parts = [_group_mean_sq(t[:, s:s + 512], gmat) for s in range(0, width, 512)]
    ms = parts[0] if len(parts) == 1 else jnp.concatenate(parts, axis=1)
    return t * lax.rsqrt(ms + RMS_EPS) * gain


def _sigmoid(x):
    return 1.0 / (1.0 + jnp.exp(-x))


def _rel_bucket(d):
    dist = jnp.maximum(d, 0)
    exact = REL_BUCKETS // 2
    log_ratio = jnp.log(jnp.maximum(dist, 1).astype(F32) / exact) / math.log(REL_MAX_DIST / exact)
    large = exact + (log_ratio * (REL_BUCKETS - exact)).astype(jnp.int32)
    return jnp.where(dist < exact, dist, jnp.minimum(large, REL_BUCKETS - 1))


def _bias_lookup(bucket, tab_ref, h):
    out = jnp.zeros(bucket.shape, F32)
    for b in range(REL_BUCKETS):
        out = jnp.where(bucket == b, tab_ref[b, h], out)
    return out


def _bias_tiles_kernel(tab_ref, o_ref):
    h = pl.program_id(0)
    r = lax.broadcasted_iota(jnp.int32, (ATT_TILE, ATT_TILE), 0)
    c = lax.broadcasted_iota(jnp.int32, (ATT_TILE, ATT_TILE), 1)
    d0 = r - c
    o_ref[0, 0] = jnp.where(d0 >= 0, _bias_lookup(_rel_bucket(d0), tab_ref, h), -jnp.inf)
    o_ref[0, 1] = _bias_lookup(_rel_bucket(d0 + ATT_TILE), tab_ref, h)


def _bias_sample_kernel(tab_ref, o_ref, *, past_len, n_new):
    h = pl.program_id(0)
    shape = o_ref.shape[1:]
    i = lax.broadcasted_iota(jnp.int32, shape, 0)
    k = lax.broadcasted_iota(jnp.int32, shape, 1)
    d = past_len + i - k
    ok = (d >= 0) & (k < past_len + n_new)
    o_ref[0] = jnp.where(ok, _bias_lookup(_rel_bucket(d), tab_ref, h), -jnp.inf)


def _build_bias(rel_bias, past_len, n_new):
    n_heads = rel_bias.shape[1]
    smem = pl.BlockSpec(memory_space=pltpu.SMEM)
    tiles = pl.pallas_call(
        _bias_tiles_kernel,
        grid=(n_heads,),
        in_specs=[smem],
        out_specs=pl.BlockSpec((1, 2, ATT_TILE, ATT_TILE), lambda h: (h, 0, 0, 0)),
        out_shape=jax.ShapeDtypeStruct((n_heads, 2, ATT_TILE, ATT_TILE), F32),
        compiler_params=_cparams(("arbitrary",)),
        name="bias_tiles",
    )(rel_bias)
    width = past_len + PAGE_SIZE
    sample = pl.pallas_call(
        functools.partial(_bias_sample_kernel, past_len=past_len, n_new=n_new),
        grid=(n_heads,),
        in_specs=[smem],
        out_specs=pl.BlockSpec((1, n_new, width), lambda h: (h, 0, 0)),
        out_shape=jax.ShapeDtypeStruct((n_heads, n_new, width), F32),
        compiler_params=_cparams(("arbitrary",)),
        name="bias_sample",
    )(rel_bias)
    return tiles, sample.reshape(n_heads * n_new, width)


def _first_half_mask(shape):
    lane = lax.broadcasted_iota(jnp.int32, shape, 1)
    return (lane & HEAD_DIM) == 0


def _ab_in_kernel(x_ref, g_ref, w_ref, qg_ref, kg_ref, gm_ref, *outs, tm, split_q):
    if split_q:
        qa_ref, qb_ref, k_ref, kb_ref, v_ref, vb_ref, u_ref, km_ref = outs
    else:
        q_ref, k_ref, v_ref, u_ref = outs
    h = _rms_rows(x_ref[...], g_ref[...]).astype(BF16)
    y = jnp.dot(h, w_ref[...], preferred_element_type=F32)
    gm = gm_ref[...]
    qn = _head_norm(y[:, 0:512], gm, qg_ref[...]) * (HEAD_DIM ** -0.5)
    kn = _head_norm(y[:, 512:1024], gm, kg_ref[...])
    v = y[:, 1024:1536]
    k_ref[...] = kn
    v_ref[...] = v
    u_ref[...] = y[:, 1536:2048]
    if split_q:
        even = _first_half_mask(qn.shape)
        qa_ref[...] = jnp.where(even, qn, 0.0).astype(BF16)
        qb_ref[...] = jnp.where(even, 0.0, qn).astype(BF16)
        kb_ref[...] = kn.astype(BF16)
        vb_ref[...] = v.astype(BF16)
        for j in range(tm // MOBA_BLOCK):
            km_ref[j] = jnp.mean(kn[j * MOBA_BLOCK:(j + 1) * MOBA_BLOCK], axis=0, keepdims=True)
    else:
        q_ref[...] = qn


def _c_in_kernel(x_ref, g_ref, w_ref, qg_ref, kg_ref, gm_ref, *outs, split_q):
    if split_q:
        qa_ref, qb_ref, k_ref, kb_ref, v_ref, vb_ref = outs
    else:
        q_ref, k_ref, v_ref = outs
    h = _rms_rows(x_ref[...], g_ref[...]).astype(BF16)
    y = jnp.dot(h, w_ref[...], preferred_element_type=F32)
    gm = gm_ref[...]
    qn = _head_norm(y[:, 0:1024], gm, qg_ref[...]) * (HEAD_DIM ** -0.5)
    kn = _head_norm(y[:, 1024:2048], gm, kg_ref[...])
    v = y[:, 2048:3072]
    k_ref[...] = kn
    v_ref[...] = v
    if split_q:
        first = _first_half_mask(qn.shape)
        qa_ref[...] = jnp.where(first, qn, 0.0).astype(BF16)
        qb_ref[...] = jnp.where(first, 0.0, qn).astype(BF16)
        kb_ref[...] = kn.astype(BF16)
        vb_ref[...] = v.astype(BF16)
    else:
        q_ref[...] = qn


def _group_matrix():
    i = jnp.arange(512) // HEAD_DIM
    return jnp.where(i[:, None] == i[None, :], 1.0 / HEAD_DIM, 0.0).astype(BF16)


def _row_spec(tm, width):
    return pl.BlockSpec((tm, width), lambda i: (i, 0))


def _const_spec(shape):
    nd = len(shape)
    return pl.BlockSpec(shape, lambda *_: (0,) * nd)


def _ab_in(x, gain, w_bf, q_gain, k_gain, gmat, *, split_q):
    n = x.shape[0]
    tm = min(512, n)
    f = lambda w: jax.ShapeDtypeStruct((n, w), F32)
    b = lambda w: jax.ShapeDtypeStruct((n, w), BF16)
    if split_q:
        out_shape = (b(512), b(512), f(512), b(512), f(512), b(512), f(512),
                     jax.ShapeDtypeStruct((n // MOBA_BLOCK, 1, 512), F32))
        out_specs = tuple(_row_spec(tm, 512) for _ in range(7)) + (
            pl.BlockSpec((tm // MOBA_BLOCK, 1, 512), lambda i: (i, 0, 0)),)
    else:
        out_shape = (f(512), f(512), f(512), f(512))
        out_specs = tuple(_row_spec(tm, 512) for _ in range(4))
    return pl.pallas_call(
        functools.partial(_ab_in_kernel, tm=tm, split_q=split_q),
        grid=(n // tm,),
        in_specs=[_row_spec(tm, D_MODEL), _const_spec((1, D_MODEL)), _const_spec((D_MODEL, 2048)),
                  _const_spec((1, 512)), _const_spec((1, 512)), _const_spec((512, 512))],
        out_specs=out_specs,
        out_shape=out_shape,
        compiler_params=_cparams(("arbitrary",)),
        name="ab_in",
    )(x, gain, w_bf, q_gain, k_gain, gmat)


def _c_in(x, gain, w_bf, q_gain, k_gain, gmat, *, split_q):
    n = x.shape[0]
    tm = min(512, n)
    f = lambda w: jax.ShapeDtypeStruct((n, w), F32)
    b = lambda w: jax.ShapeDtypeStruct((n, w), BF16)
    if split_q:
        out_shape = (b(1024), b(1024), f(1024), b(1024), f(1024), b(1024))
    else:
        out_shape = (f(1024), f(1024), f(1024))
    out_specs = tuple(_row_spec(tm, 1024) for _ in out_shape)
    return pl.pallas_call(
        functools.partial(_c_in_kernel, split_q=split_q),
        grid=(n // tm,),
        in_specs=[_row_spec(tm, D_MODEL), _const_spec((1, D_MODEL)), _const_spec((D_MODEL, 3072)),
                  _const_spec((1, 1024)), _const_spec((1, 1024)), _const_spec((512, 512))],
        out_specs=out_specs,
        out_shape=out_shape,
        compiler_params=_cparams(("arbitrary",)),
        name="c_in",
    )(x, gain, w_bf, q_gain, k_gain, gmat)


def _attn_prompt_kernel(*refs, mode, lam_init):
    if mode == "moba":
        tab_ref, qa_ref, qb_ref, k_ref, v_ref, bias_ref, km_ref, o_ref, m_ref, l_ref, acc_ref = refs
    else:
        (tab_ref, qa_ref, qb_ref, k_ref, v_ref, bias_ref, lq1_ref, lk1_ref, lq2_ref, lk2_ref,
         hg_ref, o_ref, m_ref, l_ref, acc_ref) = refs
    g = pl.program_id(1)
    qi = pl.program_id(2)
    t = ATT_TILE
    q2 = jnp.concatenate([qa_ref[0], qb_ref[0]], axis=0)
    last = REL_BUCKETS - 1
    if mode == "moba":
        row = lax.broadcasted_iota(jnp.int32, (2 * t, 1), 0)
        far = jnp.where(row < t, tab_ref[last, 2 * g], tab_ref[last, 2 * g + 1])
        tile = lambda j: bias_ref[:, j].reshape(2 * t, t)
    else:
        far = tab_ref[last, g]
        tile = lambda j: jnp.concatenate([bias_ref[0, j], bias_ref[0, j]], axis=0)

    m_ref[...] = jnp.full(m_ref.shape, M_INIT, F32)
    l_ref[...] = jnp.zeros(l_ref.shape, F32)
    acc_ref[...] = jnp.zeros(acc_ref.shape, F32)

    if mode == "moba":
        km = km_ref[0]
        nb = km.shape[0]
        km = jnp.concatenate([km, jnp.zeros((128 - nb, 128), F32)], axis=0)
        km_hi, km_lo = _split(km)
        gate = (lax.dot_general(q2, km_hi, _NT, preferred_element_type=F32)
                + lax.dot_general(q2, km_lo, _NT, preferred_element_type=F32))
        lane_i = lax.broadcasted_iota(jnp.int32, gate.shape, 1)
        lane_f = lane_i.astype(F32)
        valid = lane_i < qi
        cur = jnp.where(valid, gate, -jnp.inf)
        sel_pen = jnp.full(gate.shape, -jnp.inf, F32)
        for _ in range(MOBA_TOPK):
            top = jnp.max(cur, axis=1, keepdims=True)
            idx = jnp.min(jnp.where(cur == top, lane_f, 1e9), axis=1, keepdims=True)
            pick = lane_f == idx
            sel_pen = jnp.where(pick, jnp.where(valid, 0.0, -jnp.inf), sel_pen)
            cur = jnp.where(pick, -jnp.inf, cur)

        def block_pen(kj):
            return jnp.max(jnp.where(lane_i == kj, sel_pen, -jnp.inf), axis=1, keepdims=True)
    else:
        block_pen = None

    def process(kj, bias_add):
        start = pl.multiple_of(kj * t, t)
        kt = k_ref[0, pl.ds(start, t), :]
        vt = v_ref[0, pl.ds(start, t), :]
        s = lax.dot_general(q2, kt, _NT, preferred_element_type=F32) + bias_add
        m_prev = m_ref[...]
        m_new = jnp.maximum(m_prev, jnp.max(s, axis=1, keepdims=True))
        alpha = jnp.exp(m_prev - m_new)
        p = jnp.exp(s - m_new)
        l_ref[...] = alpha * l_ref[...] + jnp.sum(p, axis=1, keepdims=True)
        acc_ref[...] = alpha * acc_ref[...] + jnp.dot(p.astype(BF16), vt, preferred_element_type=F32)
        m_ref[...] = m_new

    def far_body(kj, carry):
        add = far if block_pen is None else far + block_pen(kj)
        process(kj, add)
        return carry

    lax.fori_loop(0, jnp.maximum(qi - 1, 0), far_body, 0)

    kn = jnp.maximum(qi - 1, 0)
    off = jnp.where(qi >= 1, 0.0, -jnp.inf)
    near = tile(1) + off
    if block_pen is not None:
        near = near + block_pen(kn)
    process(kn, near)
    process(qi, tile(0))

    out = acc_ref[...] / l_ref[...]
    if mode == "moba":
        lane = lax.broadcasted_iota(jnp.int32, (t, 128), 1)
        o_ref[0] = jnp.where(lane < HEAD_DIM, out[:t], out[t:])
    else:
        l1 = jnp.sum(lq1_ref[...] * lk1_ref[...], axis=1, keepdims=True)
        l2 = jnp.sum(lq2_ref[...] * lk2_ref[...], axis=1, keepdims=True)
        lam = jnp.exp(l1) - jnp.exp(l2) + lam_init
        o = out[:t] - lam * out[t:]
        o_ref[0] = _rms_rows(o, hg_ref[...]) * (1.0 - lam_init)


def _attn_prompt(mode, rel_bias, qa, qb, kb, vb, bias_tiles, extra, *, batch, lam_init=0.0):
    _, seq, width = qa.shape
    groups = width // 128
    nq = seq // ATT_TILE
    smem = pl.BlockSpec(memory_space=pltpu.SMEM)
    q_spec = pl.BlockSpec((1, ATT_TILE, 128), lambda b, g, i: (b, i, g))
    kv_spec = pl.BlockSpec((1, seq, 128), lambda b, g, i: (b, 0, g))
    if mode == "moba":
        (kmean,) = extra
        nb = kmean.shape[1]
        bias_spec = pl.BlockSpec((2, 2, ATT_TILE, ATT_TILE), lambda b, g, i: (g, 0, 0, 0))
        extra_specs = [pl.BlockSpec((1, nb, 128), lambda b, g, i: (b, 0, g))]
    else:
        bias_spec = pl.BlockSpec((1, 2, ATT_TILE, ATT_TILE), lambda b, g, i: (g, 0, 0, 0))
        extra_specs = [_const_spec((1, HEAD_DIM))] * 4 + [_const_spec((1, 128))]
    return pl.pallas_call(
        functools.partial(_attn_prompt_kernel, mode=mode, lam_init=lam_init),
        grid=(batch, groups, nq),
        in_specs=[smem, q_spec, q_spec, kv_spec, kv_spec, bias_spec] + extra_specs,
        out_specs=pl.BlockSpec((1, ATT_TILE, 128), lambda b, g, i: (b, i, g)),
        out_shape=jax.ShapeDtypeStruct((batch, seq, width), F32),
        scratch_shapes=[pltpu.VMEM((2 * ATT_TILE, 1), F32), pltpu.VMEM((2 * ATT_TILE, 1), F32),
                        pltpu.VMEM((2 * ATT_TILE, 128), F32)],
        compiler_params=_cparams(("arbitrary", "arbitrary", "arbitrary")),
        name=f"{mode}_prompt",
    )(rel_bias, qa, qb, kb, vb, bias_tiles, *extra)


def _page_map(b, pt_ref, *, j, li):
    return (pt_ref[b, j], li, 0, 0)


def _pad_rows(x, rows):
    return jnp.concatenate([x, jnp.zeros((rows - x.shape[0], x.shape[1]), x.dtype)], axis=0)


def _softmax_pv(scores, values):
    mx = scores[0]
    for s in scores[1:]:
        mx = jnp.maximum(mx, s)
    m = jnp.max(mx, axis=1, keepdims=True)
    tot = None
    acc = None
    for s, v in zip(scores, values):
        p = jnp.exp(s - m)
        tot = p if tot is None else tot + p
        d = jnp.dot(p.astype(BF16), v, preferred_element_type=F32)
        acc = d if acc is None else acc + d
    return acc / jnp.sum(tot, axis=1, keepdims=True)


def _moba_sample_kernel(pt_ref, q_ref, kn_ref, vn_ref, bias_ref, *rest, n_pages):
    kp = rest[:n_pages]
    vp = rest[n_pages:2 * n_pages]
    o_ref = rest[2 * n_pages]
    nq = q_ref.shape[1]
    heads = MOBA_WIDTH // HEAD_DIM
    rows = heads * nq
    q = q_ref[0]
    q_rep = jnp.concatenate([q] * heads, axis=0)
    row = lax.broadcasted_iota(jnp.int32, (rows, MOBA_WIDTH), 0)
    lane = lax.broadcasted_iota(jnp.int32, (rows, MOBA_WIDTH), 1)
    own_head = (row >> _log2(nq)) == (lane >> _log2(HEAD_DIM))
    qbd = jnp.where(own_head, q_rep, 0.0).astype(BF16)

    pages_per_block = MOBA_BLOCK // PAGE_SIZE
    n_blocks = n_pages // pages_per_block
    scores = []
    values = []
    brow = lax.broadcasted_iota(jnp.int32, (16, MOBA_WIDTH), 0)
    kmean = jnp.zeros((16, MOBA_WIDTH), F32)
    for j in range(n_pages):
        kj = kp[j][0, 0]
        kmean = kmean + jnp.where(brow == j // pages_per_block,
                                  jnp.sum(kj, axis=0, keepdims=True), 0.0)
        scores.append(lax.dot_general(qbd, kj.astype(BF16), _NT, preferred_element_type=F32))
        values.append(vp[j][0, 0].astype(BF16))
    kmean = kmean * (1.0 / MOBA_BLOCK)
    km_hi, km_lo = _split(kmean)
    gate = (lax.dot_general(qbd, km_hi, _NT, preferred_element_type=F32)
            + lax.dot_general(qbd, km_lo, _NT, preferred_element_type=F32))
    lane_i = lax.broadcasted_iota(jnp.int32, gate.shape, 1)
    lane_f = lane_i.astype(F32)
    valid = lane_i < n_blocks
    cur = jnp.where(valid, gate, -jnp.inf)
    sel_pen = jnp.full(gate.shape, -jnp.inf, F32)
    for _ in range(MOBA_TOPK):
        top = jnp.max(cur, axis=1, keepdims=True)
        idx = jnp.min(jnp.where(cur == top, lane_f, 1e9), axis=1, keepdims=True)
        pick = lane_f == idx
        sel_pen = jnp.where(pick, jnp.where(valid, 0.0, -jnp.inf), sel_pen)
        cur = jnp.where(pick, -jnp.inf, cur)

    for j in range(n_pages):
        n = j // pages_per_block
        scores[j] = scores[j] + bias_ref[:, j * PAGE_SIZE:(j + 1) * PAGE_SIZE] + sel_pen[:, n:n + 1]
    k_new = _pad_rows(kn_ref[0], PAGE_SIZE).astype(BF16)
    scores.append(lax.dot_general(qbd, k_new, _NT, preferred_element_type=F32)
                  + bias_ref[:, n_pages * PAGE_SIZE:(n_pages + 1) * PAGE_SIZE])
    values.append(_pad_rows(vn_ref[0], PAGE_SIZE).astype(BF16))

    res = jnp.where(own_head, _softmax_pv(scores, values), 0.0)
    out = res[0:nq]
    for h in range(1, heads):
        out = out + res[h * nq:(h + 1) * nq]
    o_ref[0] = out


def _diff_sample_kernel(pt_ref, q_ref, kn_ref, vn_ref, bias_ref, lq1_ref, lk1_ref, lq2_ref, lk2_ref,
                        hg_ref, *rest, n_pages, lam_init):
    kp = rest[:n_pages]
    vp = rest[n_pages:2 * n_pages]
    o_ref = rest[2 * n_pages]
    nq = q_ref.shape[1]
    heads = DIFF_HEADS
    half = heads * nq
    rows = 2 * half
    width = heads * 2 * HEAD_DIM
    q = q_ref[0]
    q_rep = jnp.concatenate([q] * (2 * heads), axis=0)
    row = lax.broadcasted_iota(jnp.int32, (rows, width), 0)
    lane = lax.broadcasted_iota(jnp.int32, (rows, width), 1)
    comp = row >> _log2(half)
    head = (row >> _log2(nq)) & (heads - 1)
    qbd = jnp.where((lane >> _log2(HEAD_DIM)) == 2 * head + comp, q_rep, 0.0).astype(BF16)

    bias = jnp.concatenate([bias_ref[...], bias_ref[...]], axis=0)
    scores = []
    values = []
    for j in range(n_pages):
        kj = kp[j][0, 0].astype(BF16)
        scores.append(lax.dot_general(qbd, kj, _NT, preferred_element_type=F32)
                      + bias[:, j * PAGE_SIZE:(j + 1) * PAGE_SIZE])
        values.append(vp[j][0, 0].astype(BF16))
    k_new = _pad_rows(kn_ref[0], PAGE_SIZE).astype(BF16)
    scores.append(lax.dot_general(qbd, k_new, _NT, preferred_element_type=F32)
                  + bias[:, n_pages * PAGE_SIZE:(n_pages + 1) * PAGE_SIZE])
    values.append(_pad_rows(vn_ref[0], PAGE_SIZE).astype(BF16))
    res = _softmax_pv(scores, values)

    l1 = jnp.sum(lq1_ref[...] * lk1_ref[...], axis=1, keepdims=True)
    l2 = jnp.sum(lq2_ref[...] * lk2_ref[...], axis=1, keepdims=True)
    lam = jnp.exp(l1) - jnp.exp(l2) + lam_init
    d = res[:half] - lam * res[half:]
    row_h = lax.broadcasted_iota(jnp.int32, (half, width), 0)
    lane_h = lax.broadcasted_iota(jnp.int32, (half, width), 1)
    own = (row_h >> _log2(nq)) == (lane_h >> _log2(2 * HEAD_DIM))
    d = jnp.where(own, d, 0.0)
    ms = jnp.sum(d * d, axis=1, keepdims=True) * (1.0 / (2 * HEAD_DIM))
    d = d * lax.rsqrt(ms + RMS_EPS)
    out = d[0:nq]
    for h in range(1, heads):
        out = out + d[h * nq:(h + 1) * nq]
    o_ref[0] = out * hg_ref[...] * (1.0 - lam_init)


def _sample_attn(mode, page_table, q, k_new, v_new, bias_rows, cache_k, cache_v, li, extra=(),
                 lam_init=0.0):
    batch, n_new, width = q.shape
    n_pages = page_table.shape[1]
    tok_spec = pl.BlockSpec((1, n_new, width), lambda b, pt: (b, 0, 0))
    const = lambda shape: pl.BlockSpec(shape, lambda b, pt: (0,) * len(shape))
    page_specs = [pl.BlockSpec((1, 1, PAGE_SIZE, width), functools.partial(_page_map, j=j, li=li))
                  for j in range(n_pages)]
    if mode == "moba":
        body = functools.partial(_moba_sample_kernel, n_pages=n_pages)
        extra_specs = []
    else:
        body = functools.partial(_diff_sample_kernel, n_pages=n_pages, lam_init=lam_init)
        extra_specs = [const((1, HEAD_DIM))] * 4 + [const((1, width))]
    grid_spec = pltpu.PrefetchScalarGridSpec(
        num_scalar_prefetch=1,
        grid=(batch,),
        in_specs=[tok_spec, tok_spec, tok_spec, const(bias_rows.shape)] + extra_specs
                 + page_specs + page_specs,
        out_specs=tok_spec,
    )
    return pl.pallas_call(
        body,
        grid_spec=grid_spec,
        out_shape=jax.ShapeDtypeStruct((batch, n_new, width), F32),
        compiler_params=_cparams(("arbitrary",)),
        name=f"{mode}_sample",
    )(page_table, q, k_new, v_new, bias_rows, *extra, *([cache_k] * n_pages), *([cache_v] * n_pages))


def _s5_tables_kernel(are_ref, aim_ref, ldt_ref, btre_ref, btim_ref, cre_ref, cim_ref,
                      xre_ref, xim_ref, zre_ref, zim_ref, wre_ref, wim_ref, vre_ref, vim_ref,
                      lre_ref, lim_ref, *, chunk):
    a_re = are_ref[...]
    a_im = aim_ref[...]
    dt = jnp.exp(ldt_ref[...])
    mag = jnp.exp(a_re * dt)
    lb_re = mag * jnp.cos(a_im * dt)
    lb_im = mag * jnp.sin(a_im * dt)
    den = a_re * a_re + a_im * a_im
    f_re = ((lb_re - 1.0) * a_re + lb_im * a_im) / den
    f_im = (lb_im * a_re - (lb_re - 1.0) * a_im) / den
    bb_re = f_re * btre_ref[...] - f_im * btim_ref[...]
    bb_im = f_re * btim_ref[...] + f_im * btre_ref[...]
    c_re = cre_ref[...]
    c_im = cim_ref[...]
    inv_mag = jnp.exp(-(a_re * dt))
    il_re = inv_mag * jnp.cos(a_im * dt)
    il_im = -inv_mag * jnp.sin(a_im * dt)

    def cmul(x_re, x_im, y_re, y_im):
        return x_re * y_re - x_im * y_im, x_re * y_im + x_im * y_re

    def put(ref, j, val):
        ref[:, j * S5_GROUP:(j + 1) * S5_GROUP, :] = val.reshape(S5_GROUPS, S5_GROUP, S5_STATE)

    one = jnp.ones_like(a_re)
    zero = jnp.zeros_like(a_re)
    pos = [(one, zero)]
    neg = [(one, zero)]
    for _ in range(chunk):
        pos.append(cmul(pos[-1][0], pos[-1][1], lb_re, lb_im))
        neg.append(cmul(neg[-1][0], neg[-1][1], il_re, il_im))
    for j in range(chunk):
        x = cmul(bb_re, bb_im, *neg[j])
        z = cmul(c_re, c_im, *pos[j])
        w = cmul(bb_re, bb_im, *pos[chunk - 1 - j])
        v = cmul(c_re, c_im, *pos[j + 1])
        put(xre_ref, j, x[0]); put(xim_ref, j, x[1])
        put(zre_ref, j, z[0]); put(zim_ref, j, z[1])
        put(wre_ref, j, w[0]); put(wim_ref, j, w[1])
        put(vre_ref, j, v[0]); put(vim_ref, j, -v[1])
    lre_ref[...] = pos[chunk][0].reshape(S5_GROUPS, S5_GROUP, S5_STATE)[:, 0:1, :]
    lim_ref[...] = pos[chunk][1].reshape(S5_GROUPS, S5_GROUP, S5_STATE)[:, 0:1, :]


def _s5_toeplitz_kernel(xre_ref, xim_ref, zre_ref, zim_ref, m_ref):
    m = _dot3_nt(xre_ref[0], zre_ref[0]) - _dot3_nt(xim_ref[0], zim_ref[0])
    r = lax.broadcasted_iota(jnp.int32, m.shape, 0) >> _log2(S5_GROUP)
    c = lax.broadcasted_iota(jnp.int32, m.shape, 1) >> _log2(S5_GROUP)
    m_ref[0] = jnp.where(c >= r, m, 0.0)


def _s5_tables(a_re, a_im, log_dt, b_re, b_im, c_re, c_im, d_skip, chunk):
    g, n, p = S5_GROUPS, S5_STATE, S5_GROUP
    kl = chunk * p
    rep = lambda x: jnp.repeat(x, p, axis=0)
    flat = lambda x: x.reshape(g * p, n)
    ins = (rep(a_re), rep(a_im), rep(jnp.broadcast_to(log_dt[:, None], (g, n))),
           flat(b_re.transpose(0, 2, 1)), flat(b_im.transpose(0, 2, 1)), flat(c_re), flat(c_im))
    big = jax.ShapeDtypeStruct((g, kl, n), F32)
    small = jax.ShapeDtypeStruct((g, 1, n), F32)
    outs = pl.pallas_call(
        functools.partial(_s5_tables_kernel, chunk=chunk),
        out_shape=(big,) * 8 + (small, small),
        compiler_params=pltpu.CompilerParams(vmem_limit_bytes=VMEM_LIMIT),
        name="s5_tables",
    )(*ins)
    x_re, x_im, z_re, z_im, w_re, w_im, v_re, v_im, l_re, l_im = outs
    gspec = pl.BlockSpec((1, kl, n), lambda i: (i, 0, 0))
    toep = pl.pallas_call(
        _s5_toeplitz_kernel,
        grid=(g,),
        in_specs=[gspec] * 4,
        out_specs=pl.BlockSpec((1, kl, kl), lambda i: (i, 0, 0)),
        out_shape=jax.ShapeDtypeStruct((g, kl, kl), F32),
        compiler_params=_cparams(("arbitrary",)),
        name="s5_toeplitz",
    )(x_re, x_im, z_re, z_im)

    def pad_lo(x):
        return jnp.pad(x, [(0, 0)] * (x.ndim - 1) + [(0, n)])

    def pad_hi(x):
        return jnp.pad(x, [(0, 0)] * (x.ndim - 1) + [(n, 0)])

    def pair_rows(w):
        w = w.reshape(g // 2, 2, kl, n)
        return jnp.concatenate([pad_lo(w[:, 0]), pad_hi(w[:, 1])], axis=1)

    def pair_pad(v):
        v = v.reshape(g // 2, 2, kl, n)
        return jnp.stack([pad_lo(v[:, 0]), pad_hi(v[:, 1])], axis=1).reshape(g, kl, 2 * n)

    def pair_lanes(l):
        l = l.reshape(g // 2, 2, 1, n)
        return jnp.concatenate([l[:, 0], l[:, 1]], axis=-1)

    d_tile = jnp.tile(d_skip, (1, chunk)).reshape(g, 1, kl)
    return dict(chunk=chunk, toep=toep, w_re=pair_rows(w_re), w_im=pair_rows(w_im),
                v_re=pair_pad(v_re), v_im=pair_pad(v_im), d=d_tile,
                l_re=pair_lanes(l_re), l_im=pair_lanes(l_im))


S5_GROUPS_PER_STEP = 4


def _s5_apply_kernel(u_ref, m_ref, wre_ref, wim_ref, vre_ref, vim_ref, d_ref, lre_ref, lim_ref,
                     s0re_ref, s0im_ref, y_ref, stre_ref, stim_ref, ere_ref, eim_ref, sre_ref, sim_ref,
                     *, n_chunks, batch):
    pairs = S5_GROUPS_PER_STEP // 2
    for k in range(pairs):
        ucat = jnp.concatenate([u_ref[2 * k], u_ref[2 * k + 1]], axis=1)
        ere_ref[k] = _dot3(ucat, wre_ref[k])
        eim_ref[k] = _dot3(ucat, wim_ref[k])
    l_re = lre_ref[...]
    l_im = lim_ref[...]

    def step(c, carry):
        s_re, s_im = carry
        rows = pl.ds(c * batch, batch)
        sre_ref[:, rows, :] = s_re
        sim_ref[:, rows, :] = s_im
        n_re = l_re * s_re - l_im * s_im + ere_ref[:, rows, :]
        n_im = l_re * s_im + l_im * s_re + eim_ref[:, rows, :]
        return n_re, n_im

    s_re, s_im = lax.fori_loop(0, n_chunks, step, (s0re_ref[...], s0im_ref[...]))
    stre_ref[...] = s_re
    stim_ref[...] = s_im
    for g in range(S5_GROUPS_PER_STEP):
        u = u_ref[g]
        y_ref[g] = (_dot3(u, m_ref[g]) + _dot3_nt(sre_ref[g // 2], vre_ref[g])
                    + _dot3_nt(sim_ref[g // 2], vim_ref[g]) + d_ref[g] * u)


def _s5_apply(u_chunks, tabs, s0_re, s0_im, *, n_chunks, batch):
    g, rows, kl = u_chunks.shape
    gs = S5_GROUPS_PER_STEP
    ps = gs // 2
    gspec = lambda shape: pl.BlockSpec((gs,) + shape, lambda i: (i, 0, 0))
    pspec = lambda shape: pl.BlockSpec((ps,) + shape, lambda i: (i, 0, 0))
    return pl.pallas_call(
        functools.partial(_s5_apply_kernel, n_chunks=n_chunks, batch=batch),
        grid=(g // gs,),
        in_specs=[gspec((rows, kl)), gspec((kl, kl)), pspec((2 * kl, 128)), pspec((2 * kl, 128)),
                  gspec((kl, 128)), gspec((kl, 128)), gspec((1, kl)), pspec((1, 128)), pspec((1, 128)),
                  pspec((batch, 128)), pspec((batch, 128))],
        out_specs=(gspec((rows, kl)), pspec((batch, 128)), pspec((batch, 128))),
        out_shape=(jax.ShapeDtypeStruct((g, rows, kl), F32),
                   jax.ShapeDtypeStruct((g // 2, batch, 128), F32),
                   jax.ShapeDtypeStruct((g // 2, batch, 128), F32)),
        scratch_shapes=[pltpu.VMEM((ps, rows, 128), F32)] * 4,
        compiler_params=_cparams(("arbitrary",)),
        name="s5_apply",
    )(u_chunks, tabs["toep"], tabs["w_re"], tabs["w_im"], tabs["v_re"], tabs["v_im"], tabs["d"],
      tabs["l_re"], tabs["l_im"], s0_re, s0_im)


def _s5(u, tabs, s0_re, s0_im, *, batch, seq):
    chunk = tabs["chunk"]
    g, p, n = S5_GROUPS, S5_GROUP, S5_STATE
    n_chunks = seq // chunk
    uc = u.reshape(batch, n_chunks, chunk, g, p).transpose(3, 1, 0, 2, 4)
    uc = uc.reshape(g, n_chunks * batch, chunk * p)
    pack = lambda s: s.reshape(batch, g // 2, 2 * n).transpose(1, 0, 2)
    y, st_re, st_im = _s5_apply(uc, tabs, pack(s0_re), pack(s0_im), n_chunks=n_chunks, batch=batch)
    y = y.reshape(g, n_chunks, batch, chunk, p).transpose(2, 1, 3, 0, 4).reshape(batch * seq, g * p)
    unpack = lambda s: s.transpose(1, 0, 2).reshape(batch, g, n)
    return y, unpack(st_re), unpack(st_im)


def _gelu_tanh(x):
    return 0.5 * x * (1.0 + jnp.tanh(math.sqrt(2.0 / math.pi) * (x + 0.044715 * (x * x * x))))


def _ab_out_kernel(x_ref, oa_ref, y_ref, wglu_ref, w_ref, o_ref):
    gl = _gelu_tanh(y_ref[...])
    gate = jnp.dot(gl.astype(BF16), wglu_ref[...], preferred_element_type=F32)
    ob = gl * _sigmoid(gate)
    cat = jnp.concatenate([oa_ref[...].astype(BF16), ob.astype(BF16)], axis=1)
    o_ref[...] = x_ref[...] + jnp.dot(cat, w_ref[...], preferred_element_type=F32)


def _c_out_kernel(x_ref, o_in_ref, w_ref, o_ref):
    o_ref[...] = x_ref[...] + jnp.dot(o_in_ref[...].astype(BF16), w_ref[...],
                                      preferred_element_type=F32)


def _ab_out(x, o_a, y, wglu_bf, w_bf):
    n = x.shape[0]
    tm = min(512, n)
    return pl.pallas_call(
        _ab_out_kernel,
        grid=(n // tm,),
        in_specs=[_row_spec(tm, D_MODEL), _row_spec(tm, 512), _row_spec(tm, 512),
                  _const_spec((512, 512)), _const_spec((D_MODEL, D_MODEL))],
        out_specs=_row_spec(tm, D_MODEL),
        out_shape=jax.ShapeDtypeStruct((n, D_MODEL), F32),
        compiler_params=_cparams(("arbitrary",)),
        name="ab_out",
    )(x, o_a, y, wglu_bf, w_bf)


def _c_out(x, o, w_bf):
    n = x.shape[0]
    tm = min(512, n)
    return pl.pallas_call(
        _c_out_kernel,
        grid=(n // tm,),
        in_specs=[_row_spec(tm, D_MODEL), _row_spec(tm, D_MODEL), _const_spec((D_MODEL, D_MODEL))],
        out_specs=_row_spec(tm, D_MODEL),
        out_shape=jax.ShapeDtypeStruct((n, D_MODEL), F32),
        compiler_params=_cparams(("arbitrary",)),
        name="c_out",
    )(x, o, w_bf)


def _ffn_kernel(*refs, tm, carry, period):
    if carry:
        x_ref, g_ref, wup_ref, cw_ref, wdn_ref, y_ref, cs_ref, hb_ref, acc_ref, halo_ref = refs
    else:
        x_ref, g_ref, wup_ref, cw_ref, wdn_ref, p1_ref, p2_ref, y_ref, gout_ref, hb_ref, acc_ref = refs
    i = pl.program_id(1)
    c = pl.program_id(2)

    @pl.when(c == 0)
    def _():
        x = x_ref[0]
        hb_ref[...] = _rms_rows(x, g_ref[...]).astype(BF16)
        acc_ref[...] = x

    gu = jnp.dot(hb_ref[...], wup_ref[0], preferred_element_type=F32)
    gate = gu[:, :FF_CHUNK]
    up = gu[:, FF_CHUNK:]
    cw = cw_ref[0]
    r1 = pltpu.roll(gate, 1, 0)
    r2 = pltpu.roll(gate, 2, 0)
    row = lax.broadcasted_iota(jnp.int32, gate.shape, 0)
    if carry:
        prev = jnp.where(i == 0, 0.0, halo_ref[c])
        p1 = jnp.where(row == 0, prev[7:8], r1)
        p2 = jnp.where(row == 0, prev[6:7], jnp.where(row == 1, prev[7:8], r2))
        tail = gate[tm - 8:tm]
        halo_ref[c] = tail
        cs_ref[0, 0] = tail
    else:
        t = row & (period - 1)
        p1 = jnp.where(t == 0, p1_ref[0], r1)
        p2 = jnp.where(t < 2, p2_ref[0], r2)
        gout_ref[0] = gate
    conv = cw[3:4] + cw[0:1] * p2 + cw[1:2] * p1 + cw[2:3] * gate
    act = (conv * _sigmoid(conv)) * up
    acc_ref[...] += jnp.dot(act.astype(BF16), wdn_ref[0], preferred_element_type=F32)

    @pl.when(c == N_FF_CHUNKS - 1)
    def _():
        y_ref[0] = acc_ref[...]


def _ffn_weights(gain, w_up, conv_w, conv_b, w_down):
    wup = w_up.astype(BF16).reshape(D_MODEL, 2, N_FF_CHUNKS, FF_CHUNK).transpose(2, 0, 1, 3)
    wup = wup.reshape(N_FF_CHUNKS, D_MODEL, 2 * FF_CHUNK)
    wdn = w_down.astype(BF16).reshape(N_FF_CHUNKS, FF_CHUNK, D_MODEL)
    cw = jnp.concatenate([conv_w, conv_b[None], jnp.zeros((4, D_FF), F32)], axis=0)
    cw = cw.reshape(8, N_FF_CHUNKS, FF_CHUNK).transpose(1, 0, 2)
    return gain.reshape(1, D_MODEL), wup, cw, wdn


def _ffn_common_specs():
    return [pl.BlockSpec((1, D_MODEL), lambda b, i, c: (0, 0)),
            pl.BlockSpec((1, D_MODEL, 2 * FF_CHUNK), lambda b, i, c: (c, 0, 0)),
            pl.BlockSpec((1, 8, FF_CHUNK), lambda b, i, c: (c, 0, 0)),
            pl.BlockSpec((1, FF_CHUNK, D_MODEL), lambda b, i, c: (c, 0, 0))]


def _ffn_prompt(x, weights, *, batch, seq):
    tm = min(1024, seq)
    x_spec = pl.BlockSpec((1, tm, D_MODEL), lambda b, i, c: (b, i, 0))
    y, cs = pl.pallas_call(
        functools.partial(_ffn_kernel, tm=tm, carry=True, period=0),
        grid=(batch, seq // tm, N_FF_CHUNKS),
        in_specs=[x_spec] + _ffn_common_specs(),
        out_specs=(x_spec, pl.BlockSpec((1, 1, 8, FF_CHUNK), lambda b, i, c: (b * (seq // tm) + i, c, 0, 0))),
        out_shape=(jax.ShapeDtypeStruct((batch, seq, D_MODEL), F32),
                   jax.ShapeDtypeStruct((batch * (seq // tm), N_FF_CHUNKS, 8, FF_CHUNK), F32)),
        scratch_shapes=[pltpu.VMEM((tm, D_MODEL), BF16), pltpu.VMEM((tm, D_MODEL), F32),
                        pltpu.VMEM((N_FF_CHUNKS, 8, FF_CHUNK), F32)],
        compiler_params=_cparams(("arbitrary", "arbitrary", "arbitrary")),
        name="ffn_prompt",
    )(x, *weights)
    cs = cs.reshape(batch, seq // tm, N_FF_CHUNKS, 8, FF_CHUNK)[:, -1]
    conv_state = cs[:, :, 6:8, :].transpose(0, 2, 1, 3).reshape(batch, 2, D_FF)
    return y, conv_state


def _ffn_sample(x, conv_buf, weights, *, batch, seq):
    n = batch * seq
    zeros = jnp.zeros((batch, seq - 2, D_FF), F32)
    p1 = jnp.concatenate([conv_buf[:, 1:2], jnp.zeros((batch, 1, D_FF), F32), zeros], axis=1)
    p2 = jnp.concatenate([conv_buf, zeros], axis=1)
    x_spec = pl.BlockSpec((1, n, D_MODEL), lambda b, i, c: (0, 0, 0))
    col_spec = pl.BlockSpec((1, n, FF_CHUNK), lambda b, i, c: (0, 0, c))
    y, gate = pl.pallas_call(
        functools.partial(_ffn_kernel, tm=n, carry=False, period=seq),
        grid=(1, 1, N_FF_CHUNKS),
        in_specs=[x_spec] + _ffn_common_specs() + [col_spec, col_spec],
        out_specs=(x_spec, col_spec),
        out_shape=(jax.ShapeDtypeStruct((1, n, D_MODEL), F32), jax.ShapeDtypeStruct((1, n, D_FF), F32)),
        scratch_shapes=[pltpu.VMEM((n, D_MODEL), BF16), pltpu.VMEM((n, D_MODEL), F32)],
        compiler_params=_cparams(("arbitrary", "arbitrary", "arbitrary")),
        name="ffn_sample",
    )(x[None], *weights, p1.reshape(1, n, D_FF), p2.reshape(1, n, D_FF))
    return y[0], gate.reshape(batch, seq, D_FF)[:, seq - 2:]


def kernel(x_prompt, x_sample, cache_moba_k, cache_moba_v, state_s5_re, state_s5_im, cache_diff_k, cache_diff_v, state_ffn_conv, page_table, rel_bias, ab_norm, w_ab_in, w_ab_out, moba_q_gain, moba_k_gain, s5_a_re, s5_a_im, s5_log_dt, s5_b_re, s5_b_im, s5_c_re, s5_c_im, s5_d, s5_w_glu, c_norm, w_c_in, w_c_out, diff_q_gain, diff_k_gain, diff_lq1, diff_lk1, diff_lq2, diff_lk2, diff_head_gain, ffn_norm, w_ffn_up, ffn_conv_w, ffn_conv_b, w_ffn_down):
    batch, seq, _ = x_prompt.shape
    dbatch, dseq, _ = x_sample.shape
    depth = ffn_norm.shape[0]
    n_pages = page_table.shape[1]
    past_len = n_pages * PAGE_SIZE
    n_phys = cache_moba_k.shape[0]

    bias_tiles, bias_rows = _build_bias(rel_bias, past_len, dseq)
    gmat = _group_matrix()
    moba_k_pages = cache_moba_k.reshape(n_phys, -1, PAGE_SIZE, MOBA_WIDTH)
    moba_v_pages = cache_moba_v.reshape(n_phys, -1, PAGE_SIZE, MOBA_WIDTH)
    diff_k_pages = cache_diff_k.reshape(n_phys, -1, PAGE_SIZE, D_MODEL)
    diff_v_pages = cache_diff_v.reshape(n_phys, -1, PAGE_SIZE, D_MODEL)
    zero_state = jnp.zeros((batch, S5_GROUPS, S5_STATE), F32)

    xp = x_prompt.reshape(batch * seq, D_MODEL)
    xs = x_sample.reshape(dbatch * dseq, D_MODEL)
    mk_p, mk_s, mv_p, mv_s = [], [], [], []
    sr_p, sr_s, si_p, si_s = [], [], [], []
    dk_p, dk_s, dv_p, dv_s = [], [], [], []
    cb_p, cb_s = [], []
    row = lambda v: v.reshape(1, -1)
    for layer in range(depth):
        li = layer // 2
        if layer % 2 == 0:
            gain = row(ab_norm[li])
            w_in = w_ab_in[li].astype(BF16)
            w_out = w_ab_out[li].astype(BF16)
            w_glu = s5_w_glu[li].astype(BF16)
            q_gain = row(jnp.tile(moba_q_gain[li], MOBA_WIDTH // HEAD_DIM))
            k_gain = row(jnp.tile(moba_k_gain[li], MOBA_WIDTH // HEAD_DIM))
            s5_args = (s5_a_re[li], s5_a_im[li], s5_log_dt[li], s5_b_re[li], s5_b_im[li],
                       s5_c_re[li], s5_c_im[li], s5_d[li])
            qa, qb, k, kb, v, vb, u, kmean = _ab_in(xp, gain, w_in, q_gain, k_gain, gmat, split_q=True)
            shp = lambda a: a.reshape(batch, seq, MOBA_WIDTH)
            o_a = _attn_prompt("moba", rel_bias, shp(qa), shp(qb), shp(kb), shp(vb), bias_tiles,
                               (kmean.reshape(batch, seq // MOBA_BLOCK, MOBA_WIDTH),), batch=batch)
            tabs = _s5_tables(*s5_args, chunk=16)
            y, s_re, s_im = _s5(u, tabs, zero_state, zero_state, batch=batch, seq=seq)
            xp = _ab_out(xp, o_a.reshape(batch * seq, MOBA_WIDTH), y, w_glu, w_out)
            mk_p.append(k.reshape(batch, seq, 8, HEAD_DIM)); mv_p.append(v.reshape(batch, seq, 8, HEAD_DIM))
            sr_p.append(s_re); si_p.append(s_im)
            q, k, v, u = _ab_in(xs, gain, w_in, q_gain, k_gain, gmat, split_q=False)
            shs = lambda a: a.reshape(dbatch, dseq, MOBA_WIDTH)
            o_a = _sample_attn("moba", page_table, shs(q), shs(k), shs(v), bias_rows,
                               moba_k_pages, moba_v_pages, li)
            tabs = _s5_tables(*s5_args, chunk=dseq)
            y, s_re, s_im = _s5(u, tabs, state_s5_re[li], state_s5_im[li], batch=dbatch, seq=dseq)
            xs = _ab_out(xs, o_a.reshape(dbatch * dseq, MOBA_WIDTH), y, w_glu, w_out)
            mk_s.append(k.reshape(dbatch, dseq, 8, HEAD_DIM)); mv_s.append(v.reshape(dbatch, dseq, 8, HEAD_DIM))
            sr_s.append(s_re); si_s.append(s_im)
        else:
            lam_init = 0.8 - 0.6 * math.exp(-0.3 * layer)
            gain = row(c_norm[li])
            w_in = w_c_in[li].astype(BF16)
            w_out = w_c_out[li].astype(BF16)
            q_gain = row(jnp.tile(diff_q_gain[li], D_MODEL // HEAD_DIM))
            k_gain = row(jnp.tile(diff_k_gain[li], D_MODEL // HEAD_DIM))
            lams = (row(diff_lq1[li]), row(diff_lk1[li]), row(diff_lq2[li]), row(diff_lk2[li]))
            head_gain = row(diff_head_gain[li])
            qa, qb, k, kb, v, vb = _c_in(xp, gain, w_in, q_gain, k_gain, gmat, split_q=True)
            shp = lambda a: a.reshape(batch, seq, D_MODEL)
            o = _attn_prompt("diff", rel_bias, shp(qa), shp(qb), shp(kb), shp(vb), bias_tiles,
                             lams + (head_gain,), batch=batch, lam_init=lam_init)
            xp = _c_out(xp, o.reshape(batch * seq, D_MODEL), w_out)
            dk_p.append(k.reshape(batch, seq, 8, 2, HEAD_DIM)); dv_p.append(v.reshape(batch, seq, 8, 2 * HEAD_DIM))
            q, k, v = _c_in(xs, gain, w_in, q_gain, k_gain, gmat, split_q=False)
            shs = lambda a: a.reshape(dbatch, dseq, D_MODEL)
            o = _sample_attn("diff", page_table, shs(q), shs(k), shs(v), bias_rows,
                             diff_k_pages, diff_v_pages, li,
                             extra=lams + (row(jnp.tile(diff_head_gain[li], DIFF_HEADS)),),
                             lam_init=lam_init)
            xs = _c_out(xs, o.reshape(dbatch * dseq, D_MODEL), w_out)
            dk_s.append(k.reshape(dbatch, dseq, 8, 2, HEAD_DIM)); dv_s.append(v.reshape(dbatch, dseq, 8, 2 * HEAD_DIM))
        weights = _ffn_weights(ffn_norm[layer], w_ffn_up[layer], ffn_conv_w[layer], ffn_conv_b[layer],
                               w_ffn_down[layer])
        yp, buf_p = _ffn_prompt(xp.reshape(batch, seq, D_MODEL), weights, batch=batch, seq=seq)
        xp = yp.reshape(batch * seq, D_MODEL)
        xs, buf_s = _ffn_sample(xs, state_ffn_conv[layer], weights, batch=dbatch, seq=dseq)
        cb_p.append(buf_p); cb_s.append(buf_s)

    st = lambda xs_, ax: jnp.stack(xs_, axis=ax)
    return (xp.reshape(batch, seq, D_MODEL), xs.reshape(dbatch, dseq, D_MODEL),
            st(mk_p, 1), st(mk_s, 1), st(mv_p, 1), st(mv_s, 1),
            st(sr_p, 0), st(sr_s, 0), st(si_p, 0), st(si_s, 0),
            st(dk_p, 1), st(dk_s, 1), st(dv_p, 1), st(dv_s, 1),
            st(cb_p, 0), st(cb_s, 0))
```

```python
import functools
import math

import jax
import jax.numpy as jnp
from jax import lax
from jax.experimental import pallas as pl
from jax.experimental.pallas import tpu as pltpu

F32 = jnp.float32
BF16 = jnp.bfloat16

D_MODEL = 1024
HEAD_DIM = 64
MOBA_WIDTH = 512
MOBA_BLOCK = 256
MOBA_TOPK = 3
S5_WIDTH = 512
S5_GROUP = 16
S5_GROUPS = 32
S5_STATE = 64
DIFF_HEADS = 8
REL_BUCKETS = 32
REL_MAX_DIST = 128
D_FF = 2816
RMS_EPS = 1e-6
PAGE_SIZE = 128

ATT_TILE = 256
FF_CHUNK = 256
N_FF_CHUNKS = D_FF // FF_CHUNK
M_INIT = -1e30
VMEM_LIMIT = 56 * 1024 * 1024

_NT = (((1,), (1,)), ((), ()))


def _log2(n):
    assert n & (n - 1) == 0
    return n.bit_length() - 1


def _cparams(sem, vmem=VMEM_LIMIT):
    return pltpu.CompilerParams(dimension_semantics=sem, vmem_limit_bytes=vmem)


def _split(a):
    hi = a.astype(BF16)
    lo = (a - hi.astype(F32)).astype(BF16)
    return hi, lo


def _dot3(a, b):
    ah, al = _split(a)
    bh, bl = _split(b)
    d = functools.partial(jnp.dot, preferred_element_type=F32)
    return d(ah, bh) + d(ah, bl) + d(al, bh)


def _dot3_nt(a, b):
    ah, al = _split(a)
    bh, bl = _split(b)
    d = functools.partial(lax.dot_general, dimension_numbers=_NT, preferred_element_type=F32)
    return d(ah, bh) + d(ah, bl) + d(al, bh)


def _rms_rows(x, g):
    ms = jnp.mean(x * x, axis=-1, keepdims=True)
    return x * lax.rsqrt(ms + RMS_EPS) * g


def _group_mean_sq(t, gmat):
    t2 = t * t
    hi, lo = _split(t2)
    return (jnp.dot(hi, gmat, preferred_element_type=F32)
            + jnp.dot(lo, gmat, preferred_element_type=F32))


def _head_norm(t, gmat, gain):
    width = t.shape[-1]
    parts = [_group_mean_sq(t[:, s:s + 512], gmat) for s in range(0, width, 512)]
    ms = parts[0] if len(parts) == 1 else jnp.concatenate(parts, axis=1)
    return t * lax.rsqrt(ms + RMS_EPS) * gain


def _sigmoid(x):
    return 1.0 / (1.0 + jnp.exp(-x))


def _rel_bucket(d):
    dist = jnp.maximum(d, 0)
    exact = REL_BUCKETS // 2
    log_ratio = jnp.log(jnp.maximum(dist, 1).astype(F32) / exact) / math.log(REL_MAX_DIST / exact)
    large = exact + (log_ratio * (REL_BUCKETS - exact)).astype(jnp.int32)
    return jnp.where(dist < exact, dist, jnp.minimum(large, REL_BUCKETS - 1))


def _bias_lookup(bucket, tab_ref, h):
    out = jnp.zeros(bucket.shape, F32)
    for b in range(REL_BUCKETS):
        out = jnp.where(bucket == b, tab_ref[b, h], out)
    return out


def _bias_tiles_kernel(tab_ref, o_ref):
    h = pl.program_id(0)
    key = lax.broadcasted_iota(jnp.int32, (ATT_TILE, ATT_TILE), 0)
    qry = lax.broadcasted_iota(jnp.int32, (ATT_TILE, ATT_TILE), 1)
    d0 = qry - key
    far = tab_ref[REL_BUCKETS - 1, h]
    o_ref[0, 0] = jnp.where(d0 >= 0, _bias_lookup(_rel_bucket(d0), tab_ref, h) - far, -jnp.inf)
    o_ref[0, 1] = _bias_lookup(_rel_bucket(d0 + ATT_TILE), tab_ref, h) - far


def _bias_sample_kernel(tab_ref, o_ref, *, past_len, n_new):
    h = pl.program_id(0)
    shape = o_ref.shape[1:]
    i = lax.broadcasted_iota(jnp.int32, shape, 0)
    k = lax.broadcasted_iota(jnp.int32, shape, 1)
    d = past_len + i - k
    ok = (d >= 0) & (k < past_len + n_new)
    o_ref[0] = jnp.where(ok, _bias_lookup(_rel_bucket(d), tab_ref, h), -jnp.inf)


def _build_bias(rel_bias, past_len, n_new):
    n_heads = rel_bias.shape[1]
    smem = pl.BlockSpec(memory_space=pltpu.SMEM)
    tiles = pl.pallas_call(
        _bias_tiles_kernel,
        grid=(n_heads,),
        in_specs=[smem],
        out_specs=pl.BlockSpec((1, 2, ATT_TILE, ATT_TILE), lambda h: (h, 0, 0, 0)),
        out_shape=jax.ShapeDtypeStruct((n_heads, 2, ATT_TILE, ATT_TILE), F32),
        compiler_params=_cparams(("arbitrary",)),
        name="bias_tiles",
    )(rel_bias)
    width = past_len + PAGE_SIZE
    sample = pl.pallas_call(
        functools.partial(_bias_sample_kernel, past_len=past_len, n_new=n_new),
        grid=(n_heads,),
        in_specs=[smem],
        out_specs=pl.BlockSpec((1, n_new, width), lambda h: (h, 0, 0)),
        out_shape=jax.ShapeDtypeStruct((n_heads, n_new, width), F32),
        compiler_params=_cparams(("arbitrary",)),
        name="bias_sample",
    )(rel_bias)
    return tiles, sample.reshape(n_heads * n_new, width)


def _first_half_mask(shape):
    lane = lax.broadcasted_iota(jnp.int32, shape, 1)
    return (lane & HEAD_DIM) == 0


def _ab_in_kernel(x_ref, g_ref, w_ref, qg_ref, kg_ref, gm_ref, *outs, tm, split_q):
    if split_q:
        qa_ref, qb_ref, k_ref, kb_ref, v_ref, vb_ref, u_ref, km_ref = outs
    else:
        q_ref, k_ref, v_ref, u_ref = outs
    h = _rms_rows(x_ref[...], g_ref[...]).astype(BF16)
    y = jnp.dot(h, w_ref[...], preferred_element_type=F32)
    gm = gm_ref[...]
    qn = _head_norm(y[:, 0:512], gm, qg_ref[...]) * (HEAD_DIM ** -0.5)
    kn = _head_norm(y[:, 512:1024], gm, kg_ref[...])
    v = y[:, 1024:1536]
    k_ref[...] = kn
    v_ref[...] = v
    u_ref[...] = y[:, 1536:2048]
    if split_q:
        even = _first_half_mask(qn.shape)
        qa_ref[...] = jnp.where(even, qn, 0.0).astype(BF16)
        qb_ref[...] = jnp.where(even, 0.0, qn).astype(BF16)
        kb_ref[...] = kn.astype(BF16)
        vb_ref[...] = v.astype(BF16)
        for j in range(tm // MOBA_BLOCK):
            km_ref[j] = jnp.mean(kn[j * MOBA_BLOCK:(j + 1) * MOBA_BLOCK], axis=0, keepdims=True)
    else:
        q_ref[...] = qn


def _c_in_kernel(x_ref, g_ref, w_ref, qg_ref, kg_ref, gm_ref, *outs, split_q):
    if split_q:
        qa_ref, qb_ref, k_ref, kb_ref, v_ref, vb_ref = outs
    else:
        q_ref, k_ref, v_ref = outs
    h = _rms_rows(x_ref[...], g_ref[...]).astype(BF16)
    y = jnp.dot(h, w_ref[...], preferred_element_type=F32)
    gm = gm_ref[...]
    qn = _head_norm(y[:, 0:1024], gm, qg_ref[...]) * (HEAD_DIM ** -0.5)
    kn = _head_norm(y[:, 1024:2048], gm, kg_ref[...])
    v = y[:, 2048:3072]
    k_ref[...] = kn
    v_ref[...] = v
    if split_q:
        first = _first_half_mask(qn.shape)
        qa_ref[...] = jnp.where(first, qn, 0.0).astype(BF16)
        qb_ref[...] = jnp.where(first, 0.0, qn).astype(BF16)
        kb_ref[...] = kn.astype(BF16)
        vb_ref[...] = v.astype(BF16)
    else:
        q_ref[...] = qn


def _group_matrix():
    i = jnp.arange(512) // HEAD_DIM
    return jnp.where(i[:, None] == i[None, :], 1.0 / HEAD_DIM, 0.0).astype(BF16)


def _row_spec(tm, width):
    return pl.BlockSpec((tm, width), lambda i: (i, 0))


def _const_spec(shape):
    nd = len(shape)
    return pl.BlockSpec(shape, lambda *_: (0,) * nd)


def _ab_in(x, gain, w_bf, q_gain, k_gain, gmat, *, split_q):
    n = x.shape[0]
    tm = min(512, n)
    f = lambda w: jax.ShapeDtypeStruct((n, w), F32)
    b = lambda w: jax.ShapeDtypeStruct((n, w), BF16)
    if split_q:
        out_shape = (b(512), b(512), f(512), b(512), f(512), b(512), f(512),
                     jax.ShapeDtypeStruct((n // MOBA_BLOCK, 1, 512), F32))
        out_specs = tuple(_row_spec(tm, 512) for _ in range(7)) + (
            pl.BlockSpec((tm // MOBA_BLOCK, 1, 512), lambda i: (i, 0, 0)),)
    else:
        out_shape = (f(512), f(512), f(512), f(512))
        out_specs = tuple(_row_spec(tm, 512) for _ in range(4))
    return pl.pallas_call(
        functools.partial(_ab_in_kernel, tm=tm, split_q=split_q),
        grid=(n // tm,),
        in_specs=[_row_spec(tm, D_MODEL), _const_spec((1, D_MODEL)), _const_spec((D_MODEL, 2048)),
                  _const_spec((1, 512)), _const_spec((1, 512)), _const_spec((512, 512))],
        out_specs=out_specs,
        out_shape=out_shape,
        compiler_params=_cparams(("arbitrary",)),
        name="ab_in",
    )(x, gain, w_bf, q_gain, k_gain, gmat)


def _c_in(x, gain, w_bf, q_gain, k_gain, gmat, *, split_q):
    n = x.shape[0]
    tm = min(512, n)
    f = lambda w: jax.ShapeDtypeStruct((n, w), F32)
    b = lambda w: jax.ShapeDtypeStruct((n, w), BF16)
    if split_q:
        out_shape = (b(1024), b(1024), f(1024), b(1024), f(1024), b(1024))
    else:
        out_shape = (f(1024), f(1024), f(1024))
    out_specs = tuple(_row_spec(tm, 1024) for _ in out_shape)
    return pl.pallas_call(
        functools.partial(_c_in_kernel, split_q=split_q),
        grid=(n // tm,),
        in_specs=[_row_spec(tm, D_MODEL), _const_spec((1, D_MODEL)), _const_spec((D_MODEL, 3072)),
                  _const_spec((1, 1024)), _const_spec((1, 1024)), _const_spec((512, 512))],
        out_specs=out_specs,
        out_shape=out_shape,
        compiler_params=_cparams(("arbitrary",)),
        name="c_in",
    )(x, gain, w_bf, q_gain, k_gain, gmat)


def _attn_prompt_kernel(*refs, mode, lam_init):
    if mode == "moba":
        qa_ref, qb_ref, k_ref, vt_ref, bias_ref, km_ref, o_ref, acc_ref, sel_ref = refs
    else:
        (qa_ref, qb_ref, k_ref, vt_ref, bias_ref, lq1_ref, lk1_ref, lq2_ref, lk2_ref,
         hg_ref, o_ref, acc_ref) = refs
    qi = pl.program_id(2)
    t = ATT_TILE
    q2 = jnp.concatenate([qa_ref[0], qb_ref[0]], axis=0)
    if mode == "moba":
        tile = lambda j: jnp.concatenate([bias_ref[0, j], bias_ref[1, j]], axis=1)
    else:
        tile = lambda j: jnp.concatenate([bias_ref[0, j], bias_ref[0, j]], axis=1)

    acc_ref[...] = jnp.zeros(acc_ref.shape, F32)

    if mode == "moba":
        km_hi, km_lo = _split(km_ref[0])
        gate = (lax.dot_general(km_hi, q2, _NT, preferred_element_type=F32)
                + lax.dot_general(km_lo, q2, _NT, preferred_element_type=F32))
        blk_i = lax.broadcasted_iota(jnp.int32, gate.shape, 0)
        blk_f = blk_i.astype(F32)
        valid = blk_i < qi
        cur = jnp.where(valid, gate, -jnp.inf)
        sel_pen = jnp.full(gate.shape, -jnp.inf, F32)
        for _ in range(MOBA_TOPK):
            top = jnp.max(cur, axis=0, keepdims=True)
            idx = jnp.min(jnp.where(cur == top, blk_f, 1e9), axis=0, keepdims=True)
            pick = blk_f == idx
            sel_pen = jnp.where(pick, jnp.where(valid, 0.0, -jnp.inf), sel_pen)
            cur = jnp.where(pick, -jnp.inf, cur)
        sel_ref[...] = sel_pen
        block_pen = lambda kj: sel_ref[pl.ds(kj, 1), :]
    else:
        block_pen = None

    def tiles(m_prev, l_prev, kjs, bias_adds):
        scores = []
        for kj, bias_add in zip(kjs, bias_adds):
            start = pl.multiple_of(kj * t, t)
            s = lax.dot_general(k_ref[0, pl.ds(start, t), :], q2, _NT,
                                preferred_element_type=F32)
            scores.append(s if bias_add is None else s + bias_add)
        parts = []
        for s in scores:
            m_t = jnp.maximum(jnp.max(s, axis=0, keepdims=True), M_INIT)
            p = jnp.exp(s - m_t)
            parts.append((m_t, jnp.sum(p, axis=0, keepdims=True), p.astype(BF16)))
        m_new = m_prev
        for m_t, _, _ in parts:
            m_new = jnp.maximum(m_new, m_t)
        alpha = jnp.exp(m_prev - m_new)
        l_new = alpha * l_prev
        acc = alpha * acc_ref[...]
        for kj, (m_t, l_t, p) in zip(kjs, parts):
            a_t = jnp.exp(m_t - m_new)
            l_new = l_new + a_t * l_t
            acc = acc + a_t * jnp.dot(vt_ref[0, 0, kj], p, preferred_element_type=F32)
        acc_ref[...] = acc
        return m_new, l_new

    def far_pen(kj):
        return None if block_pen is None else block_pen(kj)

    def far_run(first, count, m_prev, l_prev):
        kjs = [first + j for j in range(count)]
        return tiles(m_prev, l_prev, kjs, [far_pen(kj) for kj in kjs])

    n_far = jnp.maximum(qi - 1, 0)
    m0 = jnp.full((1, 2 * t), M_INIT, F32)
    l0 = jnp.zeros((1, 2 * t), F32)
    n_quads = lax.shift_right_logical(n_far, 2)
    m, l = lax.fori_loop(0, n_quads, lambda i, c: far_run(4 * i, 4, *c), (m0, l0))
    keep = lambda a, b: (a, b)
    m, l = lax.cond((n_far & 2) != 0, lambda a, b: far_run(4 * n_quads, 2, a, b), keep, m, l)
    m, l = lax.cond((n_far & 1) != 0, lambda a, b: far_run(n_far - 1, 1, a, b), keep, m, l)

    kn = jnp.maximum(qi - 1, 0)
    near = tile(1) + jnp.where(qi >= 1, 0.0, -jnp.inf)
    if block_pen is not None:
        near = near + block_pen(kn)
    m, l = tiles(m, l, [kn, qi], [near, tile(0)])

    out = acc_ref[...] / l
    if mode == "moba":
        row = lax.broadcasted_iota(jnp.int32, (128, t), 0)
        o_ref[0] = jnp.where(row < HEAD_DIM, out[:, :t], out[:, t:]).T
    else:
        l1 = jnp.sum(lq1_ref[...] * lk1_ref[...], axis=1, keepdims=True)
        l2 = jnp.sum(lq2_ref[...] * lk2_ref[...], axis=1, keepdims=True)
        lam = jnp.exp(l1) - jnp.exp(l2) + lam_init
        o = (out[:, :t] - lam * out[:, t:]).T
        o_ref[0] = _rms_rows(o, hg_ref[...]) * (1.0 - lam_init)


def _attn_prompt(mode, qa, qb, kb, vb, bias_tiles, extra, *, batch, lam_init=0.0):
    _, seq, width = qa.shape
    groups = width // 128
    nq = seq // ATT_TILE
    vt = vb.reshape(batch, nq, ATT_TILE, groups, 128).transpose(0, 3, 1, 4, 2)
    q_spec = pl.BlockSpec((1, ATT_TILE, 128), lambda b, g, i: (b, i, g))
    k_spec = pl.BlockSpec((1, seq, 128), lambda b, g, i: (b, 0, g))
    vt_spec = pl.BlockSpec((1, 1, nq, 128, ATT_TILE), lambda b, g, i: (b, g, 0, 0, 0))
    scratch = [pltpu.VMEM((128, 2 * ATT_TILE), F32)]
    if mode == "moba":
        (kmean,) = extra
        nb = kmean.shape[1]
        assert nb % 8 == 0
        bias_spec = pl.BlockSpec((2, 2, ATT_TILE, ATT_TILE), lambda b, g, i: (g, 0, 0, 0))
        extra_specs = [pl.BlockSpec((1, nb, 128), lambda b, g, i: (b, 0, g))]
        scratch.append(pltpu.VMEM((nb, 2 * ATT_TILE), F32))
    else:
        bias_spec = pl.BlockSpec((1, 2, ATT_TILE, ATT_TILE), lambda b, g, i: (g, 0, 0, 0))
        extra_specs = [_const_spec((1, HEAD_DIM))] * 4 + [_const_spec((1, 128))]
    return pl.pallas_call(
        functools.partial(_attn_prompt_kernel, mode=mode, lam_init=lam_init),
        grid=(batch, groups, nq),
        in_specs=[q_spec, q_spec, k_spec, vt_spec, bias_spec] + extra_specs,
        out_specs=pl.BlockSpec((1, ATT_TILE, 128), lambda b, g, i: (b, i, g)),
        out_shape=jax.ShapeDtypeStruct((batch, seq, width), F32),
        scratch_shapes=scratch,
        compiler_params=_cparams(("arbitrary", "arbitrary", "arbitrary")),
        name=f"{mode}_prompt",
    )(qa, qb, kb, vt, bias_tiles, *extra)


def _page_map(b, pt_ref, *, j, li):
    return (pt_ref[b, j], li, 0, 0)


def _pad_rows(x, rows):
    return jnp.concatenate([x, jnp.zeros((rows - x.shape[0], x.shape[1]), x.dtype)], axis=0)


def _softmax_pv(scores, values, transposed):
    mx = scores[0]
    for s in scores[1:]:
        mx = jnp.maximum(mx, s)
    m = jnp.max(mx, axis=1, keepdims=True)
    tot = None
    acc = None
    for s, v, tr in zip(scores, values, transposed):
        p = jnp.exp(s - m)
        tot = p if tot is None else tot + p
        if tr:
            d = lax.dot_general(p.astype(BF16), v, _NT, preferred_element_type=F32)
        else:
            d = jnp.dot(p.astype(BF16), v, preferred_element_type=F32)
        acc = d if acc is None else acc + d
    return acc / jnp.sum(tot, axis=1, keepdims=True)


def _moba_sample_kernel(pt_ref, q_ref, kn_ref, vn_ref, bias_ref, *rest, n_pages):
    kp = rest[:n_pages]
    vp = rest[n_pages:2 * n_pages]
    o_ref = rest[2 * n_pages]
    nq = q_ref.shape[1]
    heads = MOBA_WIDTH // HEAD_DIM
    rows = heads * nq
    q = q_ref[0]
    q_rep = jnp.concatenate([q] * heads, axis=0)
    row = lax.broadcasted_iota(jnp.int32, (rows, MOBA_WIDTH), 0)
    lane = lax.broadcasted_iota(jnp.int32, (rows, MOBA_WIDTH), 1)
    own_head = (row >> _log2(nq)) == (lane >> _log2(HEAD_DIM))
    qbd = jnp.where(own_head, q_rep, 0.0).astype(BF16)

    pages_per_block = MOBA_BLOCK // PAGE_SIZE
    n_blocks = n_pages // pages_per_block
    scores = [jnp.dot(qbd, kp[j][0, 0].astype(BF16), preferred_element_type=F32)
              for j in range(n_pages)]
    lane_i = lax.broadcasted_iota(jnp.int32, (rows, 128), 1)
    gate = jnp.zeros((rows, 128), F32)
    for n in range(n_blocks):
        blk = scores[n * pages_per_block]
        for j in range(n * pages_per_block + 1, (n + 1) * pages_per_block):
            blk = blk + scores[j]
        gate = jnp.where(lane_i == n, jnp.sum(blk, axis=1, keepdims=True) * (1.0 / MOBA_BLOCK), gate)
    lane_f = lane_i.astype(F32)
    valid = lane_i < n_blocks
    cur = jnp.where(valid, gate, -jnp.inf)
    sel_pen = jnp.full(gate.shape, -jnp.inf, F32)
    for _ in range(MOBA_TOPK):
        top = jnp.max(cur, axis=1, keepdims=True)
        idx = jnp.min(jnp.where(cur == top, lane_f, 1e9), axis=1, keepdims=True)
        pick = lane_f == idx
        sel_pen = jnp.where(pick, jnp.where(valid, 0.0, -jnp.inf), sel_pen)
        cur = jnp.where(pick, -jnp.inf, cur)

    for j in range(n_pages):
        n = j // pages_per_block
        scores[j] = scores[j] + bias_ref[:, j * PAGE_SIZE:(j + 1) * PAGE_SIZE] + sel_pen[:, n:n + 1]
    values = [vp[j][0, 0].astype(BF16) for j in range(n_pages)]
    k_new = _pad_rows(kn_ref[0], PAGE_SIZE).astype(BF16)
    scores.append(lax.dot_general(qbd, k_new, _NT, preferred_element_type=F32)
                  + bias_ref[:, n_pages * PAGE_SIZE:(n_pages + 1) * PAGE_SIZE])
    values.append(_pad_rows(vn_ref[0], PAGE_SIZE).astype(BF16))

    res = _softmax_pv(scores, values, [True] * n_pages + [False])
    res = jnp.where(own_head, res, 0.0)
    out = res[0:nq]
    for h in range(1, heads):
        out = out + res[h * nq:(h + 1) * nq]
    o_ref[0] = out


def _diff_sample_kernel(pt_ref, q_ref, kn_ref, vn_ref, bias_ref, lq1_ref, lk1_ref, lq2_ref, lk2_ref,
                        hg_ref, *rest, n_pages, lam_init):
    kp = rest[:n_pages]
    vp = rest[n_pages:2 * n_pages]
    o_ref = rest[2 * n_pages]
    nq = q_ref.shape[1]
    heads = DIFF_HEADS
    hw = 2 * HEAD_DIM
    l1 = jnp.sum(lq1_ref[...] * lk1_ref[...], axis=1, keepdims=True)
    l2 = jnp.sum(lq2_ref[...] * lk2_ref[...], axis=1, keepdims=True)
    lam = jnp.exp(l1) - jnp.exp(l2) + lam_init
    lane = lax.broadcasted_iota(jnp.int32, (nq, hw), 1)
    new_cols = slice(n_pages * PAGE_SIZE, (n_pages + 1) * PAGE_SIZE)
    for h in range(heads):
        cols = slice(h * hw, (h + 1) * hw)
        qh = q_ref[0, :, cols]
        q2 = jnp.concatenate([jnp.where(lane < HEAD_DIM, qh, 0.0),
                              jnp.where(lane < HEAD_DIM, 0.0, qh)], axis=0).astype(BF16)
        bias = bias_ref[2 * nq * h:2 * nq * (h + 1), :]
        scores = []
        values = []
        for j in range(n_pages):
            kj = kp[j][0, 0, cols, :].astype(BF16)
            scores.append(jnp.dot(q2, kj, preferred_element_type=F32)
                          + bias[:, j * PAGE_SIZE:(j + 1) * PAGE_SIZE])
            values.append(vp[j][0, 0, pl.ds(h, PAGE_SIZE, stride=heads), :].astype(BF16))
        k_new = _pad_rows(kn_ref[0, :, cols], PAGE_SIZE).astype(BF16)
        scores.append(lax.dot_general(q2, k_new, _NT, preferred_element_type=F32) + bias[:, new_cols])
        values.append(_pad_rows(vn_ref[0, :, cols], PAGE_SIZE).astype(BF16))
        res = _softmax_pv(scores, values, [False] * (n_pages + 1))
        d = res[:nq] - lam * res[nq:]
        o_ref[0, :, cols] = _rms_rows(d, hg_ref[...]) * (1.0 - lam_init)


def _sample_attn(mode, page_table, q, k_new, v_new, bias_rows, cache_k, cache_v, li, extra=(),
                 lam_init=0.0):
    batch, n_new, width = q.shape
    n_pages = page_table.shape[1]
    tok_spec = pl.BlockSpec((1, n_new, width), lambda b, pt: (b, 0, 0))
    const = lambda shape: pl.BlockSpec(shape, lambda b, pt: (0,) * len(shape))
    page_specs = [pl.BlockSpec((1, 1, width, PAGE_SIZE), functools.partial(_page_map, j=j, li=li))
                  for j in range(n_pages)]
    if mode == "moba":
        body = functools.partial(_moba_sample_kernel, n_pages=n_pages)
        extra_specs = []
    else:
        body = functools.partial(_diff_sample_kernel, n_pages=n_pages, lam_init=lam_init)
        extra_specs = [const((1, HEAD_DIM))] * 4 + [const((1, 2 * HEAD_DIM))]
    grid_spec = pltpu.PrefetchScalarGridSpec(
        num_scalar_prefetch=1,
        grid=(batch,),
        in_specs=[tok_spec, tok_spec, tok_spec, const(bias_rows.shape)] + extra_specs
                 + page_specs + page_specs,
        out_specs=tok_spec,
    )
    return pl.pallas_call(
        body,
        grid_spec=grid_spec,
        out_shape=jax.ShapeDtypeStruct((batch, n_new, width), F32),
        compiler_params=_cparams(("arbitrary",)),
        name=f"{mode}_sample",
    )(page_table, q, k_new, v_new, bias_rows, *extra, *([cache_k] * n_pages), *([cache_v] * n_pages))


def _s5_tables_kernel(are_ref, aim_ref, ldt_ref, btre_ref, btim_ref, cre_ref, cim_ref,
                      xre_ref, xim_ref, zre_ref, zim_ref, wre_ref, wim_ref, vre_ref, vim_ref,
                      lre_ref, lim_ref, *, chunk):
    a_re = are_ref[...]
    a_im = aim_ref[...]
    dt = jnp.exp(ldt_ref[...])
    mag = jnp.exp(a_re * dt)
    lb_re = mag * jnp.cos(a_im * dt)
    lb_im = mag * jnp.sin(a_im * dt)
    den = a_re * a_re + a_im * a_im
    f_re = ((lb_re - 1.0) * a_re + lb_im * a_im) / den
    f_im = (lb_im * a_re - (lb_re - 1.0) * a_im) / den
    bb_re = f_re * btre_ref[...] - f_im * btim_ref[...]
    bb_im = f_re * btim_ref[...] + f_im * btre_ref[...]
    c_re = cre_ref[...]
    c_im = cim_ref[...]
    inv_mag = jnp.exp(-(a_re * dt))
    il_re = inv_mag * jnp.cos(a_im * dt)
    il_im = -inv_mag * jnp.sin(a_im * dt)

    def cmul(x_re, x_im, y_re, y_im):
        return x_re * y_re - x_im * y_im, x_re * y_im + x_im * y_re

    def put(ref, j, val):
        ref[:, j * S5_GROUP:(j + 1) * S5_GROUP, :] = val.reshape(S5_GROUPS, S5_GROUP, S5_STATE)

    one = jnp.ones_like(a_re)
    zero = jnp.zeros_like(a_re)
    pos = [(one, zero)]
    neg = [(one, zero)]
    for _ in range(chunk):
        pos.append(cmul(pos[-1][0], pos[-1][1], lb_re, lb_im))
        neg.append(cmul(neg[-1][0], neg[-1][1], il_re, il_im))
    for j in range(chunk):
        x = cmul(bb_re, bb_im, *neg[j])
        z = cmul(c_re, c_im, *pos[j])
        w = cmul(bb_re, bb_im, *pos[chunk - 1 - j])
        v = cmul(c_re, c_im, *pos[j + 1])
        put(xre_ref, j, x[0]); put(xim_ref, j, x[1])
        put(zre_ref, j, z[0]); put(zim_ref, j, z[1])
        put(wre_ref, j, w[0]); put(wim_ref, j, w[1])
        put(vre_ref, j, v[0]); put(vim_ref, j, -v[1])
    lre_ref[...] = pos[chunk][0].reshape(S5_GROUPS, S5_GROUP, S5_STATE)[:, 0:1, :]
    lim_ref[...] = pos[chunk][1].reshape(S5_GROUPS, S5_GROUP, S5_STATE)[:, 0:1, :]


def _s5_toeplitz_kernel(xre_ref, xim_ref, zre_ref, zim_ref, m_ref):
    m = _dot3_nt(xre_ref[0], zre_ref[0]) - _dot3_nt(xim_ref[0], zim_ref[0])
    r = lax.broadcasted_iota(jnp.int32, m.shape, 0) >> _log2(S5_GROUP)
    c = lax.broadcasted_iota(jnp.int32, m.shape, 1) >> _log2(S5_GROUP)
    m_ref[0] = jnp.where(c >= r, m, 0.0)


def _s5_tables(a_re, a_im, log_dt, b_re, b_im, c_re, c_im, d_skip, chunk):
    g, n, p = S5_GROUPS, S5_STATE, S5_GROUP
    kl = chunk * p
    rep = lambda x: jnp.repeat(x, p, axis=0)
    flat = lambda x: x.reshape(g * p, n)
    ins = (rep(a_re), rep(a_im), rep(jnp.broadcast_to(log_dt[:, None], (g, n))),
           flat(b_re.transpose(0, 2, 1)), flat(b_im.transpose(0, 2, 1)), flat(c_re), flat(c_im))
    big = jax.ShapeDtypeStruct((g, kl, n), F32)
    small = jax.ShapeDtypeStruct((g, 1, n), F32)
    outs = pl.pallas_call(
        functools.partial(_s5_tables_kernel, chunk=chunk),
        out_shape=(big,) * 8 + (small, small),
        compiler_params=pltpu.CompilerParams(vmem_limit_bytes=VMEM_LIMIT),
        name="s5_tables",
    )(*ins)
    x_re, x_im, z_re, z_im, w_re, w_im, v_re, v_im, l_re, l_im = outs
    gspec = pl.BlockSpec((1, kl, n), lambda i: (i, 0, 0))
    toep = pl.pallas_call(
        _s5_toeplitz_kernel,
        grid=(g,),
        in_specs=[gspec] * 4,
        out_specs=pl.BlockSpec((1, kl, kl), lambda i: (i, 0, 0)),
        out_shape=jax.ShapeDtypeStruct((g, kl, kl), F32),
        compiler_params=_cparams(("arbitrary",)),
        name="s5_toeplitz",
    )(x_re, x_im, z_re, z_im)

    def pad_lo(x):
        return jnp.pad(x, [(0, 0)] * (x.ndim - 1) + [(0, n)])

    def pad_hi(x):
        return jnp.pad(x, [(0, 0)] * (x.ndim - 1) + [(n, 0)])

    def pair_rows(w):
        w = w.reshape(g // 2, 2, kl, n)
        return jnp.concatenate([pad_lo(w[:, 0]), pad_hi(w[:, 1])], axis=1)

    def pair_pad(v):
        v = v.reshape(g // 2, 2, kl, n)
        return jnp.stack([pad_lo(v[:, 0]), pad_hi(v[:, 1])], axis=1).reshape(g, kl, 2 * n)

    def pair_lanes(l):
        l = l.reshape(g // 2, 2, 1, n)
        return jnp.concatenate([l[:, 0], l[:, 1]], axis=-1)

    d_tile = jnp.tile(d_skip, (1, chunk)).reshape(g, 1, kl)
    return dict(chunk=chunk, toep=toep, w_re=pair_rows(w_re), w_im=pair_rows(w_im),
                v_re=pair_pad(v_re), v_im=pair_pad(v_im), d=d_tile,
                l_re=pair_lanes(l_re), l_im=pair_lanes(l_im))


S5_GROUPS_PER_STEP = 4


def _s5_apply_kernel(u_ref, m_ref, wre_ref, wim_ref, vre_ref, vim_ref, d_ref, lre_ref, lim_ref,
                     s0re_ref, s0im_ref, y_ref, stre_ref, stim_ref, ere_ref, eim_ref, sre_ref, sim_ref,
                     *, n_chunks, batch):
    pairs = S5_GROUPS_PER_STEP // 2
    for k in range(pairs):
        ucat = jnp.concatenate([u_ref[2 * k], u_ref[2 * k + 1]], axis=1)
        ere_ref[k] = _dot3(ucat, wre_ref[k])
        eim_ref[k] = _dot3(ucat, wim_ref[k])
    l_re = lre_ref[...]
    l_im = lim_ref[...]

    def step(c, carry):
        s_re, s_im = carry
        rows = pl.ds(c * batch, batch)
        sre_ref[:, rows, :] = s_re
        sim_ref[:, rows, :] = s_im
        n_re = l_re * s_re - l_im * s_im + ere_ref[:, rows, :]
        n_im = l_re * s_im + l_im * s_re + eim_ref[:, rows, :]
        return n_re, n_im

    s_re, s_im = lax.fori_loop(0, n_chunks, step, (s0re_ref[...], s0im_ref[...]))
    stre_ref[...] = s_re
    stim_ref[...] = s_im
    for g in range(S5_GROUPS_PER_STEP):
        u = u_ref[g]
        y_ref[g] = (_dot3(u, m_ref[g]) + _dot3_nt(sre_ref[g // 2], vre_ref[g])
                    + _dot3_nt(sim_ref[g // 2], vim_ref[g]) + d_ref[g] * u)


def _s5_apply(u_chunks, tabs, s0_re, s0_im, *, n_chunks, batch):
    g, rows, kl = u_chunks.shape
    gs = S5_GROUPS_PER_STEP
    ps = gs // 2
    gspec = lambda shape: pl.BlockSpec((gs,) + shape, lambda i: (i, 0, 0))
    pspec = lambda shape: pl.BlockSpec((ps,) + shape, lambda i: (i, 0, 0))
    return pl.pallas_call(
        functools.partial(_s5_apply_kernel, n_chunks=n_chunks, batch=batch),
        grid=(g // gs,),
        in_specs=[gspec((rows, kl)), gspec((kl, kl)), pspec((2 * kl, 128)), pspec((2 * kl, 128)),
                  gspec((kl, 128)), gspec((kl, 128)), gspec((1, kl)), pspec((1, 128)), pspec((1, 128)),
                  pspec((batch, 128)), pspec((batch, 128))],
        out_specs=(gspec((rows, kl)), pspec((batch, 128)), pspec((batch, 128))),
        out_shape=(jax.ShapeDtypeStruct((g, rows, kl), F32),
                   jax.ShapeDtypeStruct((g // 2, batch, 128), F32),
                   jax.ShapeDtypeStruct((g // 2, batch, 128), F32)),
        scratch_shapes=[pltpu.VMEM((ps, rows, 128), F32)] * 4,
        compiler_params=_cparams(("arbitrary",)),
        name="s5_apply",
    )(u_chunks, tabs["toep"], tabs["w_re"], tabs["w_im"], tabs["v_re"], tabs["v_im"], tabs["d"],
      tabs["l_re"], tabs["l_im"], s0_re, s0_im)


def _s5(u, tabs, s0_re, s0_im, *, batch, seq):
    chunk = tabs["chunk"]
    g, p, n = S5_GROUPS, S5_GROUP, S5_STATE
    n_chunks = seq // chunk
    uc = u.reshape(batch, n_chunks, chunk, g, p).transpose(3, 1, 0, 2, 4)
    uc = uc.reshape(g, n_chunks * batch, chunk * p)
    pack = lambda s: s.reshape(batch, g // 2, 2 * n).transpose(1, 0, 2)
    y, st_re, st_im = _s5_apply(uc, tabs, pack(s0_re), pack(s0_im), n_chunks=n_chunks, batch=batch)
    y = y.reshape(g, n_chunks, batch, chunk, p).transpose(2, 1, 3, 0, 4).reshape(batch * seq, g * p)
    unpack = lambda s: s.transpose(1, 0, 2).reshape(batch, g, n)
    return y, unpack(st_re), unpack(st_im)


def _gelu_tanh(x):
    return 0.5 * x * (1.0 + jnp.tanh(math.sqrt(2.0 / math.pi) * (x + 0.044715 * (x * x * x))))


def _ab_out_kernel(x_ref, oa_ref, y_ref, wglu_ref, w_ref, o_ref):
    gl = _gelu_tanh(y_ref[...])
    gate = jnp.dot(gl.astype(BF16), wglu_ref[...], preferred_element_type=F32)
    ob = gl * _sigmoid(gate)
    cat = jnp.concatenate([oa_ref[...].astype(BF16), ob.astype(BF16)], axis=1)
    o_ref[...] = x_ref[...] + jnp.dot(cat, w_ref[...], preferred_element_type=F32)


def _c_out_kernel(x_ref, o_in_ref, w_ref, o_ref):
    o_ref[...] = x_ref[...] + jnp.dot(o_in_ref[...].astype(BF16), w_ref[...],
                                      preferred_element_type=F32)


def _ab_out(x, o_a, y, wglu_bf, w_bf):
    n = x.shape[0]
    tm = min(512, n)
    return pl.pallas_call(
        _ab_out_kernel,
        grid=(n // tm,),
        in_specs=[_row_spec(tm, D_MODEL), _row_spec(tm, 512), _row_spec(tm, 512),
                  _const_spec((512, 512)), _const_spec((D_MODEL, D_MODEL))],
        out_specs=_row_spec(tm, D_MODEL),
        out_shape=jax.ShapeDtypeStruct((n, D_MODEL), F32),
        compiler_params=_cparams(("arbitrary",)),
        name="ab_out",
    )(x, o_a, y, wglu_bf, w_bf)


def _c_out(x, o, w_bf):
    n = x.shape[0]
    tm = min(512, n)
    return pl.pallas_call(
        _c_out_kernel,
        grid=(n // tm,),
        in_specs=[_row_spec(tm, D_MODEL), _row_spec(tm, D_MODEL), _const_spec((D_MODEL, D_MODEL))],
        out_specs=_row_spec(tm, D_MODEL),
        out_shape=jax.ShapeDtypeStruct((n, D_MODEL), F32),
        compiler_params=_cparams(("arbitrary",)),
        name="c_out",
    )(x, o, w_bf)


def _ffn_kernel(*refs, tm, carry, period):
    if carry:
        x_ref, g_ref, wup_ref, cw_ref, wdn_ref, y_ref, cs_ref, hb_ref, acc_ref, halo_ref = refs
    else:
        x_ref, g_ref, wup_ref, cw_ref, wdn_ref, p1_ref, p2_ref, y_ref, gout_ref, hb_ref, acc_ref = refs
    i = pl.program_id(1)
    c = pl.program_id(2)

    @pl.when(c == 0)
    def _():
        x = x_ref[0]
        hb_ref[...] = _rms_rows(x, g_ref[...]).astype(BF16)
        acc_ref[...] = x

    gu = jnp.dot(hb_ref[...], wup_ref[0], preferred_element_type=F32)
    gate = gu[:, :FF_CHUNK]
    up = gu[:, FF_CHUNK:]
    cw = cw_ref[0]
    r1 = pltpu.roll(gate, 1, 0)
    r2 = pltpu.roll(gate, 2, 0)
    row = lax.broadcasted_iota(jnp.int32, gate.shape, 0)
    if carry:
        prev = jnp.where(i == 0, 0.0, halo_ref[c])
        p1 = jnp.where(row == 0, prev[7:8], r1)
        p2 = jnp.where(row == 0, prev[6:7], jnp.where(row == 1, prev[7:8], r2))
        tail = gate[tm - 8:tm]
        halo_ref[c] = tail
        cs_ref[0, 0] = tail
    else:
        t = row & (period - 1)
        p1 = jnp.where(t == 0, p1_ref[0], r1)
        p2 = jnp.where(t < 2, p2_ref[0], r2)
        gout_ref[0] = gate
    conv = cw[3:4] + cw[0:1] * p2 + cw[1:2] * p1 + cw[2:3] * gate
    act = (conv * _sigmoid(conv)) * up
    acc_ref[...] += jnp.dot(act.astype(BF16), wdn_ref[0], preferred_element_type=F32)

    @pl.when(c == N_FF_CHUNKS - 1)
    def _():
        y_ref[0] = acc_ref[...]


def _ffn_weights(gain, w_up, conv_w, conv_b, w_down):
    wup = w_up.astype(BF16).reshape(D_MODEL, 2, N_FF_CHUNKS, FF_CHUNK).transpose(2, 0, 1, 3)
    wup = wup.reshape(N_FF_CHUNKS, D_MODEL, 2 * FF_CHUNK)
    wdn = w_down.astype(BF16).reshape(N_FF_CHUNKS, FF_CHUNK, D_MODEL)
    cw = jnp.concatenate([conv_w, conv_b[None], jnp.zeros((4, D_FF), F32)], axis=0)
    cw = cw.reshape(8, N_FF_CHUNKS, FF_CHUNK).transpose(1, 0, 2)
    return gain.reshape(1, D_MODEL), wup, cw, wdn


def _ffn_common_specs():
    return [pl.BlockSpec((1, D_MODEL), lambda b, i, c: (0, 0)),
            pl.BlockSpec((1, D_MODEL, 2 * FF_CHUNK), lambda b, i, c: (c, 0, 0)),
            pl.BlockSpec((1, 8, FF_CHUNK), lambda b, i, c: (c, 0, 0)),
            pl.BlockSpec((1, FF_CHUNK, D_MODEL), lambda b, i, c: (c, 0, 0))]


def _ffn_prompt(x, weights, *, batch, seq):
    tm = min(1024, seq)
    x_spec = pl.BlockSpec((1, tm, D_MODEL), lambda b, i, c: (b, i, 0))
    y, cs = pl.pallas_call(
        functools.partial(_ffn_kernel, tm=tm, carry=True, period=0),
        grid=(batch, seq // tm, N_FF_CHUNKS),
        in_specs=[x_spec] + _ffn_common_specs(),
        out_specs=(x_spec, pl.BlockSpec((1, 1, 8, FF_CHUNK), lambda b, i, c: (b * (seq // tm) + i, c, 0, 0))),
        out_shape=(jax.ShapeDtypeStruct((batch, seq, D_MODEL), F32),
                   jax.ShapeDtypeStruct((batch * (seq // tm), N_FF_CHUNKS, 8, FF_CHUNK), F32)),
        scratch_shapes=[pltpu.VMEM((tm, D_MODEL), BF16), pltpu.VMEM((tm, D_MODEL), F32),
                        pltpu.VMEM((N_FF_CHUNKS, 8, FF_CHUNK), F32)],
        compiler_params=_cparams(("arbitrary", "arbitrary", "arbitrary")),
        name="ffn_prompt",
    )(x, *weights)
    cs = cs.reshape(batch, seq // tm, N_FF_CHUNKS, 8, FF_CHUNK)[:, -1]
    conv_state = cs[:, :, 6:8, :].transpose(0, 2, 1, 3).reshape(batch, 2, D_FF)
    return y, conv_state


def _ffn_sample(x, conv_buf, weights, *, batch, seq):
    n = batch * seq
    zeros = jnp.zeros((batch, seq - 2, D_FF), F32)
    p1 = jnp.concatenate([conv_buf[:, 1:2], jnp.zeros((batch, 1, D_FF), F32), zeros], axis=1)
    p2 = jnp.concatenate([conv_buf, zeros], axis=1)
    x_spec = pl.BlockSpec((1, n, D_MODEL), lambda b, i, c: (0, 0, 0))
    col_spec = pl.BlockSpec((1, n, FF_CHUNK), lambda b, i, c: (0, 0, c))
    y, gate = pl.pallas_call(
        functools.partial(_ffn_kernel, tm=n, carry=False, period=seq),
        grid=(1, 1, N_FF_CHUNKS),
        in_specs=[x_spec] + _ffn_common_specs() + [col_spec, col_spec],
        out_specs=(x_spec, col_spec),
        out_shape=(jax.ShapeDtypeStruct((1, n, D_MODEL), F32), jax.ShapeDtypeStruct((1, n, D_FF), F32)),
        scratch_shapes=[pltpu.VMEM((n, D_MODEL), BF16), pltpu.VMEM((n, D_MODEL), F32)],
        compiler_params=_cparams(("arbitrary", "arbitrary", "arbitrary")),
        name="ffn_sample",
    )(x[None], *weights, p1.reshape(1, n, D_FF), p2.reshape(1, n, D_FF))
    return y[0], gate.reshape(batch, seq, D_FF)[:, seq - 2:]


def kernel(x_prompt, x_sample, cache_moba_k, cache_moba_v, state_s5_re, state_s5_im, cache_diff_k, cache_diff_v, state_ffn_conv, page_table, rel_bias, ab_norm, w_ab_in, w_ab_out, moba_q_gain, moba_k_gain, s5_a_re, s5_a_im, s5_log_dt, s5_b_re, s5_b_im, s5_c_re, s5_c_im, s5_d, s5_w_glu, c_norm, w_c_in, w_c_out, diff_q_gain, diff_k_gain, diff_lq1, diff_lk1, diff_lq2, diff_lk2, diff_head_gain, ffn_norm, w_ffn_up, ffn_conv_w, ffn_conv_b, w_ffn_down):
    batch, seq, _ = x_prompt.shape
    dbatch, dseq, _ = x_sample.shape
    depth = ffn_norm.shape[0]
    n_pages = page_table.shape[1]
    past_len = n_pages * PAGE_SIZE
    n_phys = cache_moba_k.shape[0]

    bias_tiles, bias_rows = _build_bias(rel_bias, past_len, dseq)
    gmat = _group_matrix()
    moba_k_pages = cache_moba_k.transpose(0, 1, 3, 4, 2).reshape(n_phys, -1, MOBA_WIDTH, PAGE_SIZE)
    moba_v_pages = cache_moba_v.transpose(0, 1, 3, 4, 2).reshape(n_phys, -1, MOBA_WIDTH, PAGE_SIZE)
    diff_k_pages = cache_diff_k.transpose(0, 1, 3, 4, 5, 2).reshape(n_phys, -1, D_MODEL, PAGE_SIZE)
    diff_v_pages = cache_diff_v.reshape(n_phys, -1, PAGE_SIZE * DIFF_HEADS, 2 * HEAD_DIM)
    diff_bias_rows = jnp.broadcast_to(
        bias_rows.reshape(DIFF_HEADS, 1, dseq, -1), (DIFF_HEADS, 2, dseq, bias_rows.shape[-1])
    ).reshape(2 * DIFF_HEADS * dseq, -1)
    zero_state = jnp.zeros((batch, S5_GROUPS, S5_STATE), F32)

    xp = x_prompt.reshape(batch * seq, D_MODEL)
    xs = x_sample.reshape(dbatch * dseq, D_MODEL)
    mk_p, mk_s, mv_p, mv_s = [], [], [], []
    sr_p, sr_s, si_p, si_s = [], [], [], []
    dk_p, dk_s, dv_p, dv_s = [], [], [], []
    cb_p, cb_s = [], []
    row = lambda v: v.reshape(1, -1)
    for layer in range(depth):
        li = layer // 2
        if layer % 2 == 0:
            gain = row(ab_norm[li])
            w_in = w_ab_in[li].astype(BF16)
            w_out = w_ab_out[li].astype(BF16)
            w_glu = s5_w_glu[li].astype(BF16)
            q_gain = row(jnp.tile(moba_q_gain[li], MOBA_WIDTH // HEAD_DIM))
            k_gain = row(jnp.tile(moba_k_gain[li], MOBA_WIDTH // HEAD_DIM))
            s5_args = (s5_a_re[li], s5_a_im[li], s5_log_dt[li], s5_b_re[li], s5_b_im[li],
                       s5_c_re[li], s5_c_im[li], s5_d[li])
            qa, qb, k, kb, v, vb, u, kmean = _ab_in(xp, gain, w_in, q_gain, k_gain, gmat, split_q=True)
            shp = lambda a: a.reshape(batch, seq, MOBA_WIDTH)
            o_a = _attn_prompt("moba", shp(qa), shp(qb), shp(kb), shp(vb), bias_tiles,
                               (kmean.reshape(batch, seq // MOBA_BLOCK, MOBA_WIDTH),), batch=batch)
            tabs = _s5_tables(*s5_args, chunk=16)
            y, s_re, s_im = _s5(u, tabs, zero_state, zero_state, batch=batch, seq=seq)
            xp = _ab_out(xp, o_a.reshape(batch * seq, MOBA_WIDTH), y, w_glu, w_out)
            mk_p.append(k.reshape(batch, seq, 8, HEAD_DIM)); mv_p.append(v.reshape(batch, seq, 8, HEAD_DIM))
            sr_p.append(s_re); si_p.append(s_im)
            q, k, v, u = _ab_in(xs, gain, w_in, q_gain, k_gain, gmat, split_q=False)
            shs = lambda a: a.reshape(dbatch, dseq, MOBA_WIDTH)
            o_a = _sample_attn("moba", page_table, shs(q), shs(k), shs(v), bias_rows,
                               moba_k_pages, moba_v_pages, li)
            tabs = _s5_tables(*s5_args, chunk=dseq)
            y, s_re, s_im = _s5(u, tabs, state_s5_re[li], state_s5_im[li], batch=dbatch, seq=dseq)
            xs = _ab_out(xs, o_a.reshape(dbatch * dseq, MOBA_WIDTH), y, w_glu, w_out)
            mk_s.append(k.reshape(dbatch, dseq, 8, HEAD_DIM)); mv_s.append(v.reshape(dbatch, dseq, 8, HEAD_DIM))
            sr_s.append(s_re); si_s.append(s_im)
        else:
            lam_init = 0.8 - 0.6 * math.exp(-0.3 * layer)
            gain = row(c_norm[li])
            w_in = w_c_in[li].astype(BF16)
            w_out = w_c_out[li].astype(BF16)
            q_gain = row(jnp.tile(diff_q_gain[li], D_MODEL // HEAD_DIM))
            k_gain = row(jnp.tile(diff_k_gain[li], D_MODEL // HEAD_DIM))
            lams = (row(diff_lq1[li]), row(diff_lk1[li]), row(diff_lq2[li]), row(diff_lk2[li]))
            head_gain = row(diff_head_gain[li])
            qa, qb, k, kb, v, vb = _c_in(xp, gain, w_in, q_gain, k_gain, gmat, split_q=True)
            shp = lambda a: a.reshape(batch, seq, D_MODEL)
            o = _attn_prompt("diff", shp(qa), shp(qb), shp(kb), shp(vb), bias_tiles,
                             lams + (head_gain,), batch=batch, lam_init=lam_init)
            xp = _c_out(xp, o.reshape(batch * seq, D_MODEL), w_out)
            dk_p.append(k.reshape(batch, seq, 8, 2, HEAD_DIM)); dv_p.append(v.reshape(batch, seq, 8, 2 * HEAD_DIM))
            q, k, v = _c_in(xs, gain, w_in, q_gain, k_gain, gmat, split_q=False)
            shs = lambda a: a.reshape(dbatch, dseq, D_MODEL)
            o = _sample_attn("diff", page_table, shs(q), shs(k), shs(v), diff_bias_rows,
                             diff_k_pages, diff_v_pages, li, extra=lams + (head_gain,),
                             lam_init=lam_init)
            xs = _c_out(xs, o.reshape(dbatch * dseq, D_MODEL), w_out)
            dk_s.append(k.reshape(dbatch, dseq, 8, 2, HEAD_DIM)); dv_s.append(v.reshape(dbatch, dseq, 8, 2 * HEAD_DIM))
        weights = _ffn_weights(ffn_norm[layer], w_ffn_up[layer], ffn_conv_w[layer], ffn_conv_b[layer],
                               w_ffn_down[layer])
        yp, buf_p = _ffn_prompt(xp.reshape(batch, seq, D_MODEL), weights, batch=batch, seq=seq)
        xp = yp.reshape(batch * seq, D_MODEL)
        xs, buf_s = _ffn_sample(xs, state_ffn_conv[layer], weights, batch=dbatch, seq=dseq)
        cb_p.append(buf_p); cb_s.append(buf_s)

    st = lambda xs_, ax: jnp.stack(xs_, axis=ax)
    return (xp.reshape(batch, seq, D_MODEL), xs.reshape(dbatch, dseq, D_MODEL),
            st(mk_p, 1), st(mk_s, 1), st(mv_p, 1), st(mv_s, 1),
            st(sr_p, 0), st(sr_s, 0), st(si_p, 0), st(si_s, 0),
            st(dk_p, 1), st(dk_s, 1), st(dv_p, 1), st(dv_s, 1),
            st(cb_p, 0), st(cb_s, 0))
```

```python
import functools
import math

import jax
import jax.numpy as jnp
from jax import lax
from jax.experimental import pallas as pl
from jax.experimental.pallas import tpu as pltpu

F32 = jnp.float32
BF16 = jnp.bfloat16

D_MODEL = 1024
HEAD_DIM = 64
MOBA_WIDTH = 512
MOBA_BLOCK = 256
MOBA_TOPK = 3
S5_WIDTH = 512
S5_GROUP = 16
S5_GROUPS = 32
S5_STATE = 64
DIFF_HEADS = 8
REL_BUCKETS = 32
REL_MAX_DIST = 128
D_FF = 2816
RMS_EPS = 1e-6
PAGE_SIZE = 128

ATT_TILE = 256
FF_CHUNK = 256
N_FF_CHUNKS = D_FF // FF_CHUNK
M_INIT = -1e30
LOG2E = math.log2(math.e)
VT_ROWS = 128 + 16
VMEM_LIMIT = 56 * 1024 * 1024

_NT = (((1,), (1,)), ((), ()))


def _log2(n):
    assert n & (n - 1) == 0
    return n.bit_length() - 1


def _cparams(sem, vmem=VMEM_LIMIT):
    return pltpu.CompilerParams(dimension_semantics=sem, vmem_limit_bytes=vmem)


def _split(a):
    hi = a.astype(BF16)
    lo = (a - hi.astype(F32)).astype(BF16)
    return hi, lo


def _dot3(a, b):
    ah, al = _split(a)
    bh, bl = _split(b)
    d = functools.partial(jnp.dot, preferred_element_type=F32)
    return d(ah, bh) + d(ah, bl) + d(al, bh)


def _dot3_nt(a, b):
    ah, al = _split(a)
    bh, bl = _split(b)
    d = functools.partial(lax.dot_general, dimension_numbers=_NT, preferred_element_type=F32)
    return d(ah, bh) + d(ah, bl) + d(al, bh)


def _rms_rows(x, g):
    ms = jnp.mean(x * x, axis=-1, keepdims=True)
    return x * lax.rsqrt(ms + RMS_EPS) * g


def _group_mean_sq(t, gmat):
    t2 = t * t
    hi, lo = _split(t2)
    return (jnp.dot(hi, gmat, preferred_element_type=F32)
            + jnp.dot(lo, gmat, preferred_element_type=F32))


def _head_norm(t, gmat, gain):
    slab = gmat.shape[0]
    parts = [_group_mean_sq(t[:, s:s + slab], gmat) for s in range(0, t.shape[-1], slab)]
    ms = jnp.concatenate(parts, axis=1)
    return t * lax.rsqrt(ms + RMS_EPS) * gain


def _sigmoid(x):
    return 1.0 / (1.0 + jnp.exp(-x))


def _rel_bucket(d):
    dist = jnp.maximum(d, 0)
    exact = REL_BUCKETS // 2
    log_ratio = jnp.log(jnp.maximum(dist, 1).astype(F32) / exact) / math.log(REL_MAX_DIST / exact)
    large = exact + (log_ratio * (REL_BUCKETS - exact)).astype(jnp.int32)
    return jnp.where(dist < exact, dist, jnp.minimum(large, REL_BUCKETS - 1))


def _bias_lookup(bucket, tab_ref, h):
    out = jnp.zeros(bucket.shape, F32)
    for b in range(REL_BUCKETS):
        out = jnp.where(bucket == b, tab_ref[b, h], out)
    return out


def _bias_tiles_kernel(tab_ref, o_ref):
    h = pl.program_id(0)
    key = lax.broadcasted_iota(jnp.int32, (ATT_TILE, ATT_TILE), 0)
    qry = lax.broadcasted_iota(jnp.int32, (ATT_TILE, ATT_TILE), 1)
    d0 = qry - key
    far = tab_ref[REL_BUCKETS - 1, h]
    near0 = (_bias_lookup(_rel_bucket(d0), tab_ref, h) - far) * LOG2E
    o_ref[0, 0] = jnp.where(d0 >= 0, near0, -jnp.inf)
    o_ref[0, 1] = (_bias_lookup(_rel_bucket(d0 + ATT_TILE), tab_ref, h) - far) * LOG2E


def _bias_sample_kernel(tab_ref, o_ref, *, past_len, n_new):
    h = pl.program_id(0)
    shape = o_ref.shape[1:]
    i = lax.broadcasted_iota(jnp.int32, shape, 0)
    k = lax.broadcasted_iota(jnp.int32, shape, 1)
    d = past_len + i - k
    ok = (d >= 0) & (k < past_len + n_new)
    o_ref[0] = jnp.where(ok, _bias_lookup(_rel_bucket(d), tab_ref, h), -jnp.inf)


def _build_bias(rel_bias, past_len, n_new):
    n_heads = rel_bias.shape[1]
    smem = pl.BlockSpec(memory_space=pltpu.SMEM)
    tiles = pl.pallas_call(
        _bias_tiles_kernel,
        grid=(n_heads,),
        in_specs=[smem],
        out_specs=pl.BlockSpec((1, 2, ATT_TILE, ATT_TILE), lambda h: (h, 0, 0, 0)),
        out_shape=jax.ShapeDtypeStruct((n_heads, 2, ATT_TILE, ATT_TILE), F32),
        compiler_params=_cparams(("arbitrary",)),
        name="bias_tiles",
    )(rel_bias)
    width = past_len + PAGE_SIZE
    sample = pl.pallas_call(
        functools.partial(_bias_sample_kernel, past_len=past_len, n_new=n_new),
        grid=(n_heads,),
        in_specs=[smem],
        out_specs=pl.BlockSpec((1, n_new, width), lambda h: (h, 0, 0)),
        out_shape=jax.ShapeDtypeStruct((n_heads, n_new, width), F32),
        compiler_params=_cparams(("arbitrary",)),
        name="bias_sample",
    )(rel_bias)
    return tiles, sample.reshape(n_heads * n_new, width)


def _first_half_mask(shape):
    lane = lax.broadcasted_iota(jnp.int32, shape, 1)
    return (lane & HEAD_DIM) == 0


def _in_proj_kernel(x_ref, g_ref, w_ref, qg_ref, kg_ref, gm_ref, *outs, tm, width, has_u, prompt):
    h = _rms_rows(x_ref[...], g_ref[...]).astype(BF16)
    y = jnp.dot(h, w_ref[...], preferred_element_type=F32)
    gm = gm_ref[...]
    qn = _head_norm(y[:, 0:width], gm, qg_ref[...]) * (HEAD_DIM ** -0.5)
    kn = _head_norm(y[:, width:2 * width], gm, kg_ref[...])
    v = y[:, 2 * width:3 * width]
    if not prompt:
        outs[0][...] = qn
        outs[1][...] = kn
        outs[2][...] = v
        if has_u:
            outs[3][...] = y[:, 3 * width:4 * width]
        return
    qa_ref, qb_ref, kb_ref, kt_ref, v_ref, vt_ref = outs[:6]
    first = _first_half_mask(qn.shape)
    ql = qn * LOG2E
    qa_ref[...] = jnp.where(first, ql, 0.0).astype(BF16)
    qb_ref[...] = jnp.where(first, 0.0, ql).astype(BF16)
    kb_ref[...] = kn.astype(BF16)
    kt_ref[0] = kn.T
    v_t = v.T
    if has_u:
        v_ref[0] = v_t
    else:
        v_ref[...] = v
    v_tb = v_t.astype(BF16)
    ones = jnp.ones((VT_ROWS - 128, ATT_TILE), BF16)
    for g in range(width // 128):
        for j in range(tm // ATT_TILE):
            vt_ref[0, g, j] = jnp.concatenate(
                [v_tb[g * 128:(g + 1) * 128, j * ATT_TILE:(j + 1) * ATT_TILE], ones], axis=0)
    if has_u:
        u_ref, km_ref = outs[6:]
        u_ref[...] = y[:, 3 * width:4 * width]
        for j in range(tm // MOBA_BLOCK):
            km_ref[j] = jnp.mean(kn[j * MOBA_BLOCK:(j + 1) * MOBA_BLOCK], axis=0, keepdims=True)


NORM_SLAB = 256


def _group_matrix():
    i = jnp.arange(NORM_SLAB) // HEAD_DIM
    return jnp.where(i[:, None] == i[None, :], 1.0 / HEAD_DIM, 0.0).astype(BF16)


def _row_spec(tm, width):
    return pl.BlockSpec((tm, width), lambda i: (i, 0))


def _const_spec(shape):
    nd = len(shape)
    return pl.BlockSpec(shape, lambda *_: (0,) * nd)


def _in_proj(x, gain, w_bf, q_gain, k_gain, gmat, *, width, has_u, seq=None):
    n = x.shape[0]
    prompt = seq is not None
    tm = min(512, n)
    sds = jax.ShapeDtypeStruct
    n_out = 4 if has_u else 3
    if not prompt:
        out_shape = tuple(sds((n, width), F32) for _ in range(n_out))
        out_specs = tuple(_row_spec(tm, width) for _ in range(n_out))
    else:
        batch = n // seq
        tps = seq // tm
        groups = width // 128
        feat_spec = pl.BlockSpec((1, width, tm), lambda i: (i // tps, 0, i % tps))
        out_shape = [sds((n, width), BF16), sds((n, width), BF16), sds((n, width), BF16),
                     sds((batch, width, seq), F32),
                     sds((batch, width, seq), F32) if has_u else sds((n, width), F32),
                     sds((batch, groups, seq // ATT_TILE, VT_ROWS, ATT_TILE), BF16)]
        out_specs = [_row_spec(tm, width)] * 3 + [
            feat_spec, feat_spec if has_u else _row_spec(tm, width),
            pl.BlockSpec((1, groups, tm // ATT_TILE, VT_ROWS, ATT_TILE),
                         lambda i: (i // tps, 0, i % tps, 0, 0))]
        if has_u:
            out_shape += [sds((n, width), F32), sds((n // MOBA_BLOCK, 1, width), F32)]
            out_specs += [_row_spec(tm, width),
                          pl.BlockSpec((tm // MOBA_BLOCK, 1, width), lambda i: (i, 0, 0))]
    return pl.pallas_call(
        functools.partial(_in_proj_kernel, tm=tm, width=width, has_u=has_u, prompt=prompt),
        grid=(n // tm,),
        in_specs=[_row_spec(tm, D_MODEL), _const_spec((1, D_MODEL)), _const_spec((D_MODEL, n_out * width)),
                  _const_spec((1, width)), _const_spec((1, width)), _const_spec((NORM_SLAB, NORM_SLAB))],
        out_specs=tuple(out_specs),
        out_shape=tuple(out_shape),
        compiler_params=_cparams(("arbitrary",)),
        name="ab_in" if has_u else "c_in",
    )(x, gain, w_bf, q_gain, k_gain, gmat)


def _attn_prompt_kernel(*refs, mode, lam_init):
    if mode == "moba":
        qa_ref, qb_ref, k_ref, vt_ref, bias_ref, km_ref, o_ref, acc_ref, sel_ref = refs
    else:
        (qa_ref, qb_ref, k_ref, vt_ref, bias_ref, lq1_ref, lk1_ref, lq2_ref, lk2_ref,
         hg_ref, o_ref, acc_ref) = refs
    qi = pl.program_id(2)
    t = ATT_TILE
    q2 = jnp.concatenate([qa_ref[0], qb_ref[0]], axis=0)
    if mode == "moba":
        tile = lambda j: jnp.concatenate([bias_ref[0, j], bias_ref[1, j]], axis=1)
    else:
        tile = lambda j: jnp.concatenate([bias_ref[0, j], bias_ref[0, j]], axis=1)

    acc_ref[...] = jnp.zeros(acc_ref.shape, F32)

    if mode == "moba":
        km_hi, km_lo = _split(km_ref[0])
        gate = (lax.dot_general(km_hi, q2, _NT, preferred_element_type=F32)
                + lax.dot_general(km_lo, q2, _NT, preferred_element_type=F32))
        blk_i = lax.broadcasted_iota(jnp.int32, gate.shape, 0)
        blk_f = blk_i.astype(F32)
        valid = blk_i < qi
        cur = jnp.where(valid, gate, -jnp.inf)
        sel_pen = jnp.full(gate.shape, -jnp.inf, F32)
        for _ in range(MOBA_TOPK):
            top = jnp.max(cur, axis=0, keepdims=True)
            idx = jnp.min(jnp.where(cur == top, blk_f, 1e9), axis=0, keepdims=True)
            pick = blk_f == idx
            sel_pen = jnp.where(pick, jnp.where(valid, 0.0, -jnp.inf), sel_pen)
            cur = jnp.where(pick, -jnp.inf, cur)
        sel_ref[...] = sel_pen
        block_pen = lambda kj: sel_ref[pl.ds(kj, 1), :]
    else:
        block_pen = None

    def tiles(m_prev, kjs, bias_adds):
        scores = []
        for kj, bias_add in zip(kjs, bias_adds):
            start = pl.multiple_of(kj * t, t)
            s = lax.dot_general(k_ref[0, pl.ds(start, t), :], q2, _NT,
                                preferred_element_type=F32)
            scores.append(s if bias_add is None else s + bias_add)
        parts = []
        for s in scores:
            m_t = jnp.maximum(jnp.max(s, axis=0, keepdims=True), M_INIT)
            parts.append((m_t, jnp.exp2(s - m_t).astype(BF16)))
        m_new = m_prev
        for m_t, _ in parts:
            m_new = jnp.maximum(m_new, m_t)
        acc = jnp.exp2(m_prev - m_new) * acc_ref[...]
        for kj, (m_t, p) in zip(kjs, parts):
            acc = acc + jnp.exp2(m_t - m_new) * jnp.dot(vt_ref[0, 0, kj], p,
                                                        preferred_element_type=F32)
        acc_ref[...] = acc
        return m_new

    def far_pen(kj):
        return None if block_pen is None else block_pen(kj)

    def far_run(first, count, m_prev):
        kjs = [first + j for j in range(count)]
        return tiles(m_prev, kjs, [far_pen(kj) for kj in kjs])

    n_far = jnp.maximum(qi - 1, 0)
    m = jnp.full((1, 2 * t), M_INIT, F32)
    n_quads = lax.shift_right_logical(n_far, 2)
    m = lax.fori_loop(0, n_quads, lambda i, c: far_run(4 * i, 4, c), m)
    m = lax.cond((n_far & 2) != 0, lambda a: far_run(4 * n_quads, 2, a), lambda a: a, m)
    m = lax.cond((n_far & 1) != 0, lambda a: far_run(n_far - 1, 1, a), lambda a: a, m)

    kn = jnp.maximum(qi - 1, 0)
    near = tile(1) + jnp.where(qi >= 1, 0.0, -jnp.inf)
    if block_pen is not None:
        near = near + block_pen(kn)
    tiles(m, [kn, qi], [near, tile(0)])

    out = acc_ref[0:128, :] / acc_ref[128:129, :]
    if mode == "moba":
        row = lax.broadcasted_iota(jnp.int32, (128, t), 0)
        o_ref[0] = jnp.where(row < HEAD_DIM, out[:, :t], out[:, t:]).T
    else:
        l1 = jnp.sum(lq1_ref[...] * lk1_ref[...], axis=1, keepdims=True)
        l2 = jnp.sum(lq2_ref[...] * lk2_ref[...], axis=1, keepdims=True)
        lam = jnp.exp(l1) - jnp.exp(l2) + lam_init
        o = (out[:, :t] - lam * out[:, t:]).T
        o_ref[0] = _rms_rows(o, hg_ref[...]) * (1.0 - lam_init)


def _attn_prompt(mode, qa, qb, kb, vt, bias_tiles, extra, *, batch, lam_init=0.0):
    _, seq, width = qa.shape
    groups = width // 128
    nq = seq // ATT_TILE
    q_spec = pl.BlockSpec((1, ATT_TILE, 128), lambda b, g, i: (b, i, g))
    k_spec = pl.BlockSpec((1, seq, 128), lambda b, g, i: (b, 0, g))
    vt_spec = pl.BlockSpec((1, 1, nq, VT_ROWS, ATT_TILE), lambda b, g, i: (b, g, 0, 0, 0))
    scratch = [pltpu.VMEM((VT_ROWS, 2 * ATT_TILE), F32)]
    if mode == "moba":
        (kmean,) = extra
        nb = kmean.shape[1]
        assert nb % 8 == 0
        bias_spec = pl.BlockSpec((2, 2, ATT_TILE, ATT_TILE), lambda b, g, i: (g, 0, 0, 0))
        extra_specs = [pl.BlockSpec((1, nb, 128), lambda b, g, i: (b, 0, g))]
        scratch.append(pltpu.VMEM((nb, 2 * ATT_TILE), F32))
    else:
        bias_spec = pl.BlockSpec((1, 2, ATT_TILE, ATT_TILE), lambda b, g, i: (g, 0, 0, 0))
        extra_specs = [_const_spec((1, HEAD_DIM))] * 4 + [_const_spec((1, 128))]
    return pl.pallas_call(
        functools.partial(_attn_prompt_kernel, mode=mode, lam_init=lam_init),
        grid=(batch, groups, nq),
        in_specs=[q_spec, q_spec, k_spec, vt_spec, bias_spec] + extra_specs,
        out_specs=pl.BlockSpec((1, ATT_TILE, 128), lambda b, g, i: (b, i, g)),
        out_shape=jax.ShapeDtypeStruct((batch, seq, width), F32),
        scratch_shapes=scratch,
        compiler_params=_cparams(("arbitrary", "arbitrary", "arbitrary")),
        name=f"{mode}_prompt",
    )(qa, qb, kb, vt, bias_tiles, *extra)


def _page_map(b, pt_ref, *, j, li):
    return (pt_ref[b, j], li, 0, 0)


def _pad_rows(x, rows):
    return jnp.concatenate([x, jnp.zeros((rows - x.shape[0], x.shape[1]), x.dtype)], axis=0)


def _softmax_pv(scores, values, transposed):
    mx = scores[0]
    for s in scores[1:]:
        mx = jnp.maximum(mx, s)
    m = jnp.max(mx, axis=1, keepdims=True)
    tot = None
    acc = None
    for s, v, tr in zip(scores, values, transposed):
        p = jnp.exp(s - m)
        tot = p if tot is None else tot + p
        if tr:
            d = lax.dot_general(p.astype(BF16), v, _NT, preferred_element_type=F32)
        else:
            d = jnp.dot(p.astype(BF16), v, preferred_element_type=F32)
        acc = d if acc is None else acc + d
    return acc / jnp.sum(tot, axis=1, keepdims=True)


def _moba_sample_kernel(pt_ref, q_ref, kn_ref, vn_ref, bias_ref, *rest, n_pages):
    kp = rest[:n_pages]
    vp = rest[n_pages:2 * n_pages]
    o_ref = rest[2 * n_pages]
    nq = q_ref.shape[1]
    heads = MOBA_WIDTH // HEAD_DIM
    rows = heads * nq
    q = q_ref[0]
    q_rep = jnp.concatenate([q] * heads, axis=0)
    row = lax.broadcasted_iota(jnp.int32, (rows, MOBA_WIDTH), 0)
    lane = lax.broadcasted_iota(jnp.int32, (rows, MOBA_WIDTH), 1)
    own_head = (row >> _log2(nq)) == (lane >> _log2(HEAD_DIM))
    qbd = jnp.where(own_head, q_rep, 0.0).astype(BF16)

    pages_per_block = MOBA_BLOCK // PAGE_SIZE
    n_blocks = n_pages // pages_per_block
    scores = [jnp.dot(qbd, kp[j][0, 0].astype(BF16), preferred_element_type=F32)
              for j in range(n_pages)]
    lane_i = lax.broadcasted_iota(jnp.int32, (rows, 128), 1)
    gate = jnp.zeros((rows, 128), F32)
    for n in range(n_blocks):
        blk = scores[n * pages_per_block]
        for j in range(n * pages_per_block + 1, (n + 1) * pages_per_block):
            blk = blk + scores[j]
        gate = jnp.where(lane_i == n, jnp.sum(blk, axis=1, keepdims=True) * (1.0 / MOBA_BLOCK), gate)
    lane_f = lane_i.astype(F32)
    valid = lane_i < n_blocks
    cur = jnp.where(valid, gate, -jnp.inf)
    sel_pen = jnp.full(gate.shape, -jnp.inf, F32)
    for _ in range(MOBA_TOPK):
        top = jnp.max(cur, axis=1, keepdims=True)
        idx = jnp.min(jnp.where(cur == top, lane_f, 1e9), axis=1, keepdims=True)
        pick = lane_f == idx
        sel_pen = jnp.where(pick, jnp.where(valid, 0.0, -jnp.inf), sel_pen)
        cur = jnp.where(pick, -jnp.inf, cur)

    for j in range(n_pages):
        n = j // pages_per_block
        scores[j] = scores[j] + bias_ref[:, j * PAGE_SIZE:(j + 1) * PAGE_SIZE] + sel_pen[:, n:n + 1]
    values = [vp[j][0, 0].astype(BF16) for j in range(n_pages)]
    k_new = _pad_rows(kn_ref[0], PAGE_SIZE).astype(BF16)
    scores.append(lax.dot_general(qbd, k_new, _NT, preferred_element_type=F32)
                  + bias_ref[:, n_pages * PAGE_SIZE:(n_pages + 1) * PAGE_SIZE])
    values.append(_pad_rows(vn_ref[0], PAGE_SIZE).astype(BF16))

    res = _softmax_pv(scores, values, [True] * n_pages + [False])
    res = jnp.where(own_head, res, 0.0)
    out = res[0:nq]
    for h in range(1, heads):
        out = out + res[h * nq:(h + 1) * nq]
    o_ref[0] = out


def _diff_sample_kernel(pt_ref, q_ref, kn_ref, vn_ref, bias_ref, lq1_ref, lk1_ref, lq2_ref, lk2_ref,
                        hg_ref, *rest, n_pages, lam_init):
    kp = rest[:n_pages]
    vp = rest[n_pages:2 * n_pages]
    o_ref = rest[2 * n_pages]
    nq = q_ref.shape[1]
    heads = DIFF_HEADS
    hw = 2 * HEAD_DIM
    l1 = jnp.sum(lq1_ref[...] * lk1_ref[...], axis=1, keepdims=True)
    l2 = jnp.sum(lq2_ref[...] * lk2_ref[...], axis=1, keepdims=True)
    lam = jnp.exp(l1) - jnp.exp(l2) + lam_init
    lane = lax.broadcasted_iota(jnp.int32, (nq, hw), 1)
    new_cols = slice(n_pages * PAGE_SIZE, (n_pages + 1) * PAGE_SIZE)
    for h in range(heads):
        cols = slice(h * hw, (h + 1) * hw)
        qh = q_ref[0, :, cols]
        q2 = jnp.concatenate([jnp.where(lane < HEAD_DIM, qh, 0.0),
                              jnp.where(lane < HEAD_DIM, 0.0, qh)], axis=0).astype(BF16)
        bias = bias_ref[2 * nq * h:2 * nq * (h + 1), :]
        scores = []
        values = []
        for j in range(n_pages):
            kj = kp[j][0, 0, cols, :].astype(BF16)
            scores.append(jnp.dot(q2, kj, preferred_element_type=F32)
                          + bias[:, j * PAGE_SIZE:(j + 1) * PAGE_SIZE])
            values.append(vp[j][0, 0, pl.ds(h, PAGE_SIZE, stride=heads), :].astype(BF16))
        k_new = _pad_rows(kn_ref[0, :, cols], PAGE_SIZE).astype(BF16)
        scores.append(lax.dot_general(q2, k_new, _NT, preferred_element_type=F32) + bias[:, new_cols])
        values.append(_pad_rows(vn_ref[0, :, cols], PAGE_SIZE).astype(BF16))
        res = _softmax_pv(scores, values, [False] * (n_pages + 1))
        d = res[:nq] - lam * res[nq:]
        o_ref[0, :, cols] = _rms_rows(d, hg_ref[...]) * (1.0 - lam_init)


def _sample_attn(mode, page_table, q, k_new, v_new, bias_rows, cache_k, cache_v, li, extra=(),
                 lam_init=0.0):
    batch, n_new, width = q.shape
    n_pages = page_table.shape[1]
    tok_spec = pl.BlockSpec((1, n_new, width), lambda b, pt: (b, 0, 0))
    const = lambda shape: pl.BlockSpec(shape, lambda b, pt: (0,) * len(shape))
    page_specs = [pl.BlockSpec((1, 1, width, PAGE_SIZE), functools.partial(_page_map, j=j, li=li))
                  for j in range(n_pages)]
    if mode == "moba":
        body = functools.partial(_moba_sample_kernel, n_pages=n_pages)
        extra_specs = []
    else:
        body = functools.partial(_diff_sample_kernel, n_pages=n_pages, lam_init=lam_init)
        extra_specs = [const((1, HEAD_DIM))] * 4 + [const((1, 2 * HEAD_DIM))]
    grid_spec = pltpu.PrefetchScalarGridSpec(
        num_scalar_prefetch=1,
        grid=(batch,),
        in_specs=[tok_spec, tok_spec, tok_spec, const(bias_rows.shape)] + extra_specs
                 + page_specs + page_specs,
        out_specs=tok_spec,
    )
    return pl.pallas_call(
        body,
        grid_spec=grid_spec,
        out_shape=jax.ShapeDtypeStruct((batch, n_new, width), F32),
        compiler_params=_cparams(("arbitrary",)),
        name=f"{mode}_sample",
    )(page_table, q, k_new, v_new, bias_rows, *extra, *([cache_k] * n_pages), *([cache_v] * n_pages))


def _s5_tables_kernel(are_ref, aim_ref, ldt_ref, btre_ref, btim_ref, cre_ref, cim_ref,
                      xre_ref, xim_ref, zre_ref, zim_ref, wre_ref, wim_ref, vre_ref, vim_ref,
                      lre_ref, lim_ref, *, chunk):
    a_re = are_ref[...]
    a_im = aim_ref[...]
    dt = jnp.exp(ldt_ref[...])
    mag = jnp.exp(a_re * dt)
    lb_re = mag * jnp.cos(a_im * dt)
    lb_im = mag * jnp.sin(a_im * dt)
    den = a_re * a_re + a_im * a_im
    f_re = ((lb_re - 1.0) * a_re + lb_im * a_im) / den
    f_im = (lb_im * a_re - (lb_re - 1.0) * a_im) / den
    bb_re = f_re * btre_ref[...] - f_im * btim_ref[...]
    bb_im = f_re * btim_ref[...] + f_im * btre_ref[...]
    c_re = cre_ref[...]
    c_im = cim_ref[...]
    inv_mag = jnp.exp(-(a_re * dt))
    il_re = inv_mag * jnp.cos(a_im * dt)
    il_im = -inv_mag * jnp.sin(a_im * dt)

    def cmul(x_re, x_im, y_re, y_im):
        return x_re * y_re - x_im * y_im, x_re * y_im + x_im * y_re

    def put(ref, j, val):
        ref[:, j * S5_GROUP:(j + 1) * S5_GROUP, :] = val.reshape(S5_GROUPS, S5_GROUP, S5_STATE)

    one = jnp.ones_like(a_re)
    zero = jnp.zeros_like(a_re)
    pos = [(one, zero)]
    neg = [(one, zero)]
    for _ in range(chunk):
        pos.append(cmul(pos[-1][0], pos[-1][1], lb_re, lb_im))
        neg.append(cmul(neg[-1][0], neg[-1][1], il_re, il_im))
    for j in range(chunk):
        x = cmul(bb_re, bb_im, *neg[j])
        z = cmul(c_re, c_im, *pos[j])
        w = cmul(bb_re, bb_im, *pos[chunk - 1 - j])
        v = cmul(c_re, c_im, *pos[j + 1])
        put(xre_ref, j, x[0]); put(xim_ref, j, x[1])
        put(zre_ref, j, z[0]); put(zim_ref, j, z[1])
        put(wre_ref, j, w[0]); put(wim_ref, j, w[1])
        put(vre_ref, j, v[0]); put(vim_ref, j, -v[1])
    lre_ref[...] = pos[chunk][0].reshape(S5_GROUPS, S5_GROUP, S5_STATE)[:, 0:1, :]
    lim_ref[...] = pos[chunk][1].reshape(S5_GROUPS, S5_GROUP, S5_STATE)[:, 0:1, :]


def _s5_toeplitz_kernel(xre_ref, xim_ref, zre_ref, zim_ref, m_ref):
    m = _dot3_nt(xre_ref[0], zre_ref[0]) - _dot3_nt(xim_ref[0], zim_ref[0])
    r = lax.broadcasted_iota(jnp.int32, m.shape, 0) >> _log2(S5_GROUP)
    c = lax.broadcasted_iota(jnp.int32, m.shape, 1) >> _log2(S5_GROUP)
    m_ref[0] = jnp.where(c >= r, m, 0.0)


def _s5_tables(a_re, a_im, log_dt, b_re, b_im, c_re, c_im, d_skip, chunk):
    g, n, p = S5_GROUPS, S5_STATE, S5_GROUP
    kl = chunk * p
    rep = lambda x: jnp.repeat(x, p, axis=0)
    flat = lambda x: x.reshape(g * p, n)
    ins = (rep(a_re), rep(a_im), rep(jnp.broadcast_to(log_dt[:, None], (g, n))),
           flat(b_re.transpose(0, 2, 1)), flat(b_im.transpose(0, 2, 1)), flat(c_re), flat(c_im))
    big = jax.ShapeDtypeStruct((g, kl, n), F32)
    small = jax.ShapeDtypeStruct((g, 1, n), F32)
    outs = pl.pallas_call(
        functools.partial(_s5_tables_kernel, chunk=chunk),
        out_shape=(big,) * 8 + (small, small),
        compiler_params=pltpu.CompilerParams(vmem_limit_bytes=VMEM_LIMIT),
        name="s5_tables",
    )(*ins)
    x_re, x_im, z_re, z_im, w_re, w_im, v_re, v_im, l_re, l_im = outs
    gspec = pl.BlockSpec((1, kl, n), lambda i: (i, 0, 0))
    toep = pl.pallas_call(
        _s5_toeplitz_kernel,
        grid=(g,),
        in_specs=[gspec] * 4,
        out_specs=pl.BlockSpec((1, kl, kl), lambda i: (i, 0, 0)),
        out_shape=jax.ShapeDtypeStruct((g, kl, kl), F32),
        compiler_params=_cparams(("arbitrary",)),
        name="s5_toeplitz",
    )(x_re, x_im, z_re, z_im)

    def pad_lo(x):
        return jnp.pad(x, [(0, 0)] * (x.ndim - 1) + [(0, n)])

    def pad_hi(x):
        return jnp.pad(x, [(0, 0)] * (x.ndim - 1) + [(n, 0)])

    def pair_rows(w):
        w = w.reshape(g // 2, 2, kl, n)
        return jnp.concatenate([pad_lo(w[:, 0]), pad_hi(w[:, 1])], axis=1)

    def pair_pad(v):
        v = v.reshape(g // 2, 2, kl, n)
        return jnp.stack([pad_lo(v[:, 0]), pad_hi(v[:, 1])], axis=1).reshape(g, kl, 2 * n)

    def pair_lanes(l):
        l = l.reshape(g // 2, 2, 1, n)
        return jnp.concatenate([l[:, 0], l[:, 1]], axis=-1)

    d_tile = jnp.tile(d_skip, (1, chunk)).reshape(g, 1, kl)
    return dict(chunk=chunk, toep=toep, w_re=pair_rows(w_re), w_im=pair_rows(w_im),
                v_re=pair_pad(v_re), v_im=pair_pad(v_im), d=d_tile,
                l_re=pair_lanes(l_re), l_im=pair_lanes(l_im))


S5_GROUPS_PER_STEP = 4


def _s5_apply_kernel(u_ref, m_ref, wre_ref, wim_ref, vre_ref, vim_ref, d_ref, lre_ref, lim_ref,
                     s0re_ref, s0im_ref, y_ref, stre_ref, stim_ref, ere_ref, eim_ref, sre_ref, sim_ref,
                     *, n_chunks, batch):
    pairs = S5_GROUPS_PER_STEP // 2
    for k in range(pairs):
        ucat = jnp.concatenate([u_ref[2 * k], u_ref[2 * k + 1]], axis=1)
        ere_ref[k] = _dot3(ucat, wre_ref[k])
        eim_ref[k] = _dot3(ucat, wim_ref[k])
    l_re = lre_ref[...]
    l_im = lim_ref[...]

    def step(c, carry):
        s_re, s_im = carry
        rows = pl.ds(c * batch, batch)
        sre_ref[:, rows, :] = s_re
        sim_ref[:, rows, :] = s_im
        n_re = l_re * s_re - l_im * s_im + ere_ref[:, rows, :]
        n_im = l_re * s_im + l_im * s_re + eim_ref[:, rows, :]
        return n_re, n_im

    s_re, s_im = lax.fori_loop(0, n_chunks, step, (s0re_ref[...], s0im_ref[...]))
    stre_ref[...] = s_re
    stim_ref[...] = s_im
    for g in range(S5_GROUPS_PER_STEP):
        u = u_ref[g]
        y_ref[g] = (_dot3(u, m_ref[g]) + _dot3_nt(sre_ref[g // 2], vre_ref[g])
                    + _dot3_nt(sim_ref[g // 2], vim_ref[g]) + d_ref[g] * u)


def _s5_apply(u_chunks, tabs, s0_re, s0_im, *, n_chunks, batch):
    g, rows, kl = u_chunks.shape
    gs = S5_GROUPS_PER_STEP
    ps = gs // 2
    gspec = lambda shape: pl.BlockSpec((gs,) + shape, lambda i: (i, 0, 0))
    pspec = lambda shape: pl.BlockSpec((ps,) + shape, lambda i: (i, 0, 0))
    return pl.pallas_call(
        functools.partial(_s5_apply_kernel, n_chunks=n_chunks, batch=batch),
        grid=(g // gs,),
        in_specs=[gspec((rows, kl)), gspec((kl, kl)), pspec((2 * kl, 128)), pspec((2 * kl, 128)),
                  gspec((kl, 128)), gspec((kl, 128)), gspec((1, kl)), pspec((1, 128)), pspec((1, 128)),
                  pspec((batch, 128)), pspec((batch, 128))],
        out_specs=(gspec((rows, kl)), pspec((batch, 128)), pspec((batch, 128))),
        out_shape=(jax.ShapeDtypeStruct((g, rows, kl), F32),
                   jax.ShapeDtypeStruct((g // 2, batch, 128), F32),
                   jax.ShapeDtypeStruct((g // 2, batch, 128), F32)),
        scratch_shapes=[pltpu.VMEM((ps, rows, 128), F32)] * 4,
        compiler_params=_cparams(("arbitrary",)),
        name="s5_apply",
    )(u_chunks, tabs["toep"], tabs["w_re"], tabs["w_im"], tabs["v_re"], tabs["v_im"], tabs["d"],
      tabs["l_re"], tabs["l_im"], s0_re, s0_im)


def _s5(u, tabs, s0_re, s0_im, *, batch, seq):
    chunk = tabs["chunk"]
    g, p, n = S5_GROUPS, S5_GROUP, S5_STATE
    n_chunks = seq // chunk
    uc = u.reshape(batch, n_chunks, chunk, g, p).transpose(3, 1, 0, 2, 4)
    uc = uc.reshape(g, n_chunks * batch, chunk * p)
    pack = lambda s: s.reshape(batch, g // 2, 2 * n).transpose(1, 0, 2)
    y, st_re, st_im = _s5_apply(uc, tabs, pack(s0_re), pack(s0_im), n_chunks=n_chunks, batch=batch)
    y = y.reshape(g, n_chunks, batch, chunk, p).transpose(2, 1, 3, 0, 4).reshape(batch * seq, g * p)
    unpack = lambda s: s.transpose(1, 0, 2).reshape(batch, g, n)
    return y, unpack(st_re), unpack(st_im)


def _gelu_tanh(x):
    return 0.5 * x * (1.0 + jnp.tanh(math.sqrt(2.0 / math.pi) * (x + 0.044715 * (x * x * x))))


def _ab_out_kernel(x_ref, oa_ref, y_ref, wglu_ref, w_ref, o_ref):
    gl = _gelu_tanh(y_ref[...])
    gate = jnp.dot(gl.astype(BF16), wglu_ref[...], preferred_element_type=F32)
    ob = gl * _sigmoid(gate)
    cat = jnp.concatenate([oa_ref[...].astype(BF16), ob.astype(BF16)], axis=1)
    o_ref[...] = x_ref[...] + jnp.dot(cat, w_ref[...], preferred_element_type=F32)


def _c_out_kernel(x_ref, o_in_ref, w_ref, o_ref):
    o_ref[...] = x_ref[...] + jnp.dot(o_in_ref[...].astype(BF16), w_ref[...],
                                      preferred_element_type=F32)


def _ab_out(x, o_a, y, wglu_bf, w_bf):
    n = x.shape[0]
    tm = min(512, n)
    return pl.pallas_call(
        _ab_out_kernel,
        grid=(n // tm,),
        in_specs=[_row_spec(tm, D_MODEL), _row_spec(tm, 512), _row_spec(tm, 512),
                  _const_spec((512, 512)), _const_spec((D_MODEL, D_MODEL))],
        out_specs=_row_spec(tm, D_MODEL),
        out_shape=jax.ShapeDtypeStruct((n, D_MODEL), F32),
        compiler_params=_cparams(("arbitrary",)),
        name="ab_out",
    )(x, o_a, y, wglu_bf, w_bf)


def _c_out(x, o, w_bf):
    n = x.shape[0]
    tm = min(512, n)
    return pl.pallas_call(
        _c_out_kernel,
        grid=(n // tm,),
        in_specs=[_row_spec(tm, D_MODEL), _row_spec(tm, D_MODEL), _const_spec((D_MODEL, D_MODEL))],
        out_specs=_row_spec(tm, D_MODEL),
        out_shape=jax.ShapeDtypeStruct((n, D_MODEL), F32),
        compiler_params=_cparams(("arbitrary",)),
        name="c_out",
    )(x, o, w_bf)


def _ffn_kernel(*refs, tm, carry, period):
    if carry:
        x_ref, g_ref, wup_ref, cw_ref, wdn_ref, y_ref, cs_ref, hb_ref, acc_ref, halo_ref = refs
    else:
        x_ref, g_ref, wup_ref, cw_ref, wdn_ref, p1_ref, p2_ref, y_ref, gout_ref, hb_ref, acc_ref = refs
    i = pl.program_id(1)
    c = pl.program_id(2)

    @pl.when(c == 0)
    def _():
        x = x_ref[0]
        hb_ref[...] = _rms_rows(x, g_ref[...]).astype(BF16)
        acc_ref[...] = x

    gu = jnp.dot(hb_ref[...], wup_ref[0], preferred_element_type=F32)
    gate = gu[:, :FF_CHUNK]
    up = gu[:, FF_CHUNK:]
    cw = cw_ref[0]
    r1 = pltpu.roll(gate, 1, 0)
    r2 = pltpu.roll(gate, 2, 0)
    row = lax.broadcasted_iota(jnp.int32, gate.shape, 0)
    if carry:
        prev = jnp.where(i == 0, 0.0, halo_ref[c])
        p1 = jnp.where(row == 0, prev[7:8], r1)
        p2 = jnp.where(row == 0, prev[6:7], jnp.where(row == 1, prev[7:8], r2))
        tail = gate[tm - 8:tm]
        halo_ref[c] = tail
        cs_ref[0, 0] = tail
    else:
        t = row & (period - 1)
        rep = lambda b: jnp.broadcast_to(b[:, None, :], (tm // period, period, b.shape[-1])
                                         ).reshape(tm, b.shape[-1])
        old0 = rep(p1_ref[...])
        old1 = rep(p2_ref[...])
        p1 = jnp.where(t == 0, old1, r1)
        p2 = jnp.where(t == 0, old0, jnp.where(t == 1, old1, r2))
        gout_ref[0] = gate
    conv = cw[3:4] + cw[0:1] * p2 + cw[1:2] * p1 + cw[2:3] * gate
    act = (conv * _sigmoid(conv)) * up
    acc_ref[...] += jnp.dot(act.astype(BF16), wdn_ref[0], preferred_element_type=F32)

    @pl.when(c == N_FF_CHUNKS - 1)
    def _():
        y_ref[0] = acc_ref[...]


def _ffn_weights(gain, w_up, conv_w, conv_b, w_down):
    wup = w_up.astype(BF16).reshape(D_MODEL, 2, N_FF_CHUNKS, FF_CHUNK).transpose(2, 0, 1, 3)
    wup = wup.reshape(N_FF_CHUNKS, D_MODEL, 2 * FF_CHUNK)
    wdn = w_down.astype(BF16).reshape(N_FF_CHUNKS, FF_CHUNK, D_MODEL)
    cw = jnp.concatenate([conv_w, conv_b[None], jnp.zeros((4, D_FF), F32)], axis=0)
    cw = cw.reshape(8, N_FF_CHUNKS, FF_CHUNK).transpose(1, 0, 2)
    return gain.reshape(1, D_MODEL), wup, cw, wdn


def _ffn_common_specs():
    return [pl.BlockSpec((1, D_MODEL), lambda b, i, c: (0, 0)),
            pl.BlockSpec((1, D_MODEL, 2 * FF_CHUNK), lambda b, i, c: (c, 0, 0)),
            pl.BlockSpec((1, 8, FF_CHUNK), lambda b, i, c: (c, 0, 0)),
            pl.BlockSpec((1, FF_CHUNK, D_MODEL), lambda b, i, c: (c, 0, 0))]


def _ffn_prompt(x, weights, *, batch, seq):
    tm = min(1024, seq)
    x_spec = pl.BlockSpec((1, tm, D_MODEL), lambda b, i, c: (b, i, 0))
    y, cs = pl.pallas_call(
        functools.partial(_ffn_kernel, tm=tm, carry=True, period=0),
        grid=(batch, seq // tm, N_FF_CHUNKS),
        in_specs=[x_spec] + _ffn_common_specs(),
        out_specs=(x_spec, pl.BlockSpec((1, 1, 8, FF_CHUNK), lambda b, i, c: (b * (seq // tm) + i, c, 0, 0))),
        out_shape=(jax.ShapeDtypeStruct((batch, seq, D_MODEL), F32),
                   jax.ShapeDtypeStruct((batch * (seq // tm), N_FF_CHUNKS, 8, FF_CHUNK), F32)),
        scratch_shapes=[pltpu.VMEM((tm, D_MODEL), BF16), pltpu.VMEM((tm, D_MODEL), F32),
                        pltpu.VMEM((N_FF_CHUNKS, 8, FF_CHUNK), F32)],
        compiler_params=_cparams(("arbitrary", "arbitrary", "arbitrary")),
        name="ffn_prompt",
    )(x, *weights)
    cs = cs.reshape(batch, seq // tm, N_FF_CHUNKS, 8, FF_CHUNK)[:, -1]
    conv_state = cs[:, :, 6:8, :].transpose(0, 2, 1, 3).reshape(batch, 2, D_FF)
    return y, conv_state


def _ffn_sample(x, conv_buf, weights, *, batch, seq):
    n = batch * seq
    x_spec = pl.BlockSpec((1, n, D_MODEL), lambda b, i, c: (0, 0, 0))
    col_spec = pl.BlockSpec((1, n, FF_CHUNK), lambda b, i, c: (0, 0, c))
    buf_spec = pl.BlockSpec((batch, FF_CHUNK), lambda b, i, c: (0, c))
    y, gate = pl.pallas_call(
        functools.partial(_ffn_kernel, tm=n, carry=False, period=seq),
        grid=(1, 1, N_FF_CHUNKS),
        in_specs=[x_spec] + _ffn_common_specs() + [buf_spec, buf_spec],
        out_specs=(x_spec, col_spec),
        out_shape=(jax.ShapeDtypeStruct((1, n, D_MODEL), F32), jax.ShapeDtypeStruct((1, n, D_FF), F32)),
        scratch_shapes=[pltpu.VMEM((n, D_MODEL), BF16), pltpu.VMEM((n, D_MODEL), F32)],
        compiler_params=_cparams(("arbitrary", "arbitrary", "arbitrary")),
        name="ffn_sample",
    )(x[None], *weights, conv_buf[:, 0], conv_buf[:, 1])
    return y[0], gate.reshape(batch, seq, D_FF)[:, seq - 2:]


def kernel(x_prompt, x_sample, cache_moba_k, cache_moba_v, state_s5_re, state_s5_im, cache_diff_k, cache_diff_v, state_ffn_conv, page_table, rel_bias, ab_norm, w_ab_in, w_ab_out, moba_q_gain, moba_k_gain, s5_a_re, s5_a_im, s5_log_dt, s5_b_re, s5_b_im, s5_c_re, s5_c_im, s5_d, s5_w_glu, c_norm, w_c_in, w_c_out, diff_q_gain, diff_k_gain, diff_lq1, diff_lk1, diff_lq2, diff_lk2, diff_head_gain, ffn_norm, w_ffn_up, ffn_conv_w, ffn_conv_b, w_ffn_down):
    batch, seq, _ = x_prompt.shape
    dbatch, dseq, _ = x_sample.shape
    depth = ffn_norm.shape[0]
    n_pages = page_table.shape[1]
    past_len = n_pages * PAGE_SIZE
    n_phys = cache_moba_k.shape[0]

    bias_tiles, bias_rows = _build_bias(rel_bias, past_len, dseq)
    gmat = _group_matrix()
    moba_k_pages = cache_moba_k.transpose(0, 1, 3, 4, 2).reshape(n_phys, -1, MOBA_WIDTH, PAGE_SIZE)
    moba_v_pages = cache_moba_v.transpose(0, 1, 3, 4, 2).reshape(n_phys, -1, MOBA_WIDTH, PAGE_SIZE)
    diff_k_pages = cache_diff_k.transpose(0, 1, 3, 4, 5, 2).reshape(n_phys, -1, D_MODEL, PAGE_SIZE)
    diff_v_pages = cache_diff_v.reshape(n_phys, -1, PAGE_SIZE * DIFF_HEADS, 2 * HEAD_DIM)
    diff_bias_rows = jnp.broadcast_to(
        bias_rows.reshape(DIFF_HEADS, 1, dseq, -1), (DIFF_HEADS, 2, dseq, bias_rows.shape[-1])
    ).reshape(2 * DIFF_HEADS * dseq, -1)
    zero_state = jnp.zeros((batch, S5_GROUPS, S5_STATE), F32)

    xp = x_prompt.reshape(batch * seq, D_MODEL)
    xs = x_sample.reshape(dbatch * dseq, D_MODEL)
    mk_p, mk_s, mv_p, mv_s = [], [], [], []
    sr_p, sr_s, si_p, si_s = [], [], [], []
    dk_p, dk_s, dv_p, dv_s = [], [], [], []
    cb_p, cb_s = [], []
    row = lambda v: v.reshape(1, -1)
    for layer in range(depth):
        li = layer // 2
        if layer % 2 == 0:
            gain = row(ab_norm[li])
            w_in = w_ab_in[li].astype(BF16)
            w_out = w_ab_out[li].astype(BF16)
            w_glu = s5_w_glu[li].astype(BF16)
            q_gain = row(jnp.tile(moba_q_gain[li], MOBA_WIDTH // HEAD_DIM))
            k_gain = row(jnp.tile(moba_k_gain[li], MOBA_WIDTH // HEAD_DIM))
            s5_args = (s5_a_re[li], s5_a_im[li], s5_log_dt[li], s5_b_re[li], s5_b_im[li],
                       s5_c_re[li], s5_c_im[li], s5_d[li])
            qa, qb, kb, kt, v_t, vt, u, kmean = _in_proj(xp, gain, w_in, q_gain, k_gain, gmat,
                                                         width=MOBA_WIDTH, has_u=True, seq=seq)
            shp = lambda a: a.reshape(batch, seq, MOBA_WIDTH)
            o_a = _attn_prompt("moba", shp(qa), shp(qb), shp(kb), vt, bias_tiles,
                               (kmean.reshape(batch, seq // MOBA_BLOCK, MOBA_WIDTH),), batch=batch)
            tabs = _s5_tables(*s5_args, chunk=16)
            y, s_re, s_im = _s5(u, tabs, zero_state, zero_state, batch=batch, seq=seq)
            xp = _ab_out(xp, o_a.reshape(batch * seq, MOBA_WIDTH), y, w_glu, w_out)
            rows_of = lambda a: a.reshape(batch, 8, HEAD_DIM, seq).transpose(0, 3, 1, 2)
            mk_p.append(rows_of(kt)); mv_p.append(rows_of(v_t))
            sr_p.append(s_re); si_p.append(s_im)
            q, k, v, u = _in_proj(xs, gain, w_in, q_gain, k_gain, gmat, width=MOBA_WIDTH, has_u=True)
            shs = lambda a: a.reshape(dbatch, dseq, MOBA_WIDTH)
            o_a = _sample_attn("moba", page_table, shs(q), shs(k), shs(v), bias_rows,
                               moba_k_pages, moba_v_pages, li)
            tabs = _s5_tables(*s5_args, chunk=dseq)
            y, s_re, s_im = _s5(u, tabs, state_s5_re[li], state_s5_im[li], batch=dbatch, seq=dseq)
            xs = _ab_out(xs, o_a.reshape(dbatch * dseq, MOBA_WIDTH), y, w_glu, w_out)
            mk_s.append(k.reshape(dbatch, dseq, 8, HEAD_DIM)); mv_s.append(v.reshape(dbatch, dseq, 8, HEAD_DIM))
            sr_s.append(s_re); si_s.append(s_im)
        else:
            lam_init = 0.8 - 0.6 * math.exp(-0.3 * layer)
            gain = row(c_norm[li])
            w_in = w_c_in[li].astype(BF16)
            w_out = w_c_out[li].astype(BF16)
            q_gain = row(jnp.tile(diff_q_gain[li], D_MODEL // HEAD_DIM))
            k_gain = row(jnp.tile(diff_k_gain[li], D_MODEL // HEAD_DIM))
            lams = (row(diff_lq1[li]), row(diff_lk1[li]), row(diff_lq2[li]), row(diff_lk2[li]))
            head_gain = row(diff_head_gain[li])
            qa, qb, kb, kt, v, vt = _in_proj(xp, gain, w_in, q_gain, k_gain, gmat,
                                             width=D_MODEL, has_u=False, seq=seq)
            shp = lambda a: a.reshape(batch, seq, D_MODEL)
            o = _attn_prompt("diff", shp(qa), shp(qb), shp(kb), vt, bias_tiles,
                             lams + (head_gain,), batch=batch, lam_init=lam_init)
            xp = _c_out(xp, o.reshape(batch * seq, D_MODEL), w_out)
            dk_p.append(kt.reshape(batch, 8, 2, HEAD_DIM, seq).transpose(0, 4, 1, 2, 3))
            dv_p.append(v.reshape(batch, seq, 8, 2 * HEAD_DIM))
            q, k, v = _in_proj(xs, gain, w_in, q_gain, k_gain, gmat, width=D_MODEL, has_u=False)
            shs = lambda a: a.reshape(dbatch, dseq, D_MODEL)
            o = _sample_attn("diff", page_table, shs(q), shs(k), shs(v), diff_bias_rows,
                             diff_k_pages, diff_v_pages, li, extra=lams + (head_gain,),
                             lam_init=lam_init)
            xs = _c_out(xs, o.reshape(dbatch * dseq, D_MODEL), w_out)
            dk_s.append(k.reshape(dbatch, dseq, 8, 2, HEAD_DIM)); dv_s.append(v.reshape(dbatch, dseq, 8, 2 * HEAD_DIM))
        weights = _ffn_weights(ffn_norm[layer], w_ffn_up[layer], ffn_conv_w[layer], ffn_conv_b[layer],
                               w_ffn_down[layer])
        yp, buf_p = _ffn_prompt(xp.reshape(batch, seq, D_MODEL), weights, batch=batch, seq=seq)
        xp = yp.reshape(batch * seq, D_MODEL)
        xs, buf_s = _ffn_sample(xs, state_ffn_conv[layer], weights, batch=dbatch, seq=dseq)
        cb_p.append(buf_p); cb_s.append(buf_s)

    st = lambda xs_, ax: jnp.stack(xs_, axis=ax)
    return (xp.reshape(batch, seq, D_MODEL), xs.reshape(dbatch, dseq, D_MODEL),
            st(mk_p, 1), st(mk_s, 1), st(mv_p, 1), st(mv_s, 1),
            st(sr_p, 0), st(sr_s, 0), st(si_p, 0), st(si_s, 0),
            st(dk_p, 1), st(dk_s, 1), st(dv_p, 1), st(dv_s, 1),
            st(cb_p, 0), st(cb_s, 0))
```

```python
import functools
import math

import jax
import jax.numpy as jnp
from jax import lax
from jax.experimental import pallas as pl
from jax.experimental.pallas import tpu as pltpu

F32 = jnp.float32
BF16 = jnp.bfloat16

D_MODEL = 1024
HEAD_DIM = 64
MOBA_WIDTH = 512
MOBA_BLOCK = 256
MOBA_TOPK = 3
S5_WIDTH = 512
S5_GROUP = 16
S5_GROUPS = 32
S5_STATE = 64
DIFF_HEADS = 8
REL_BUCKETS = 32
REL_MAX_DIST = 128
D_FF = 2816
RMS_EPS = 1e-6
PAGE_SIZE = 128

ATT_TILE = 256
FF_CHUNK = 256
N_FF_CHUNKS = D_FF // FF_CHUNK
M_INIT = -1e30
LOG2E = math.log2(math.e)
VT_ROWS = 128 + 16
VMEM_LIMIT = 56 * 1024 * 1024

_NT = (((1,), (1,)), ((), ()))


def _log2(n):
    assert n & (n - 1) == 0
    return n.bit_length() - 1


def _cparams(sem, vmem=VMEM_LIMIT):
    return pltpu.CompilerParams(dimension_semantics=sem, vmem_limit_bytes=vmem)


def _split(a):
    hi = a.astype(BF16)
    lo = (a - hi.astype(F32)).astype(BF16)
    return hi, lo


def _dot3(a, b):
    ah, al = _split(a)
    bh, bl = _split(b)
    d = functools.partial(jnp.dot, preferred_element_type=F32)
    return d(ah, bh) + d(ah, bl) + d(al, bh)


def _dot3_nt(a, b):
    ah, al = _split(a)
    bh, bl = _split(b)
    d = functools.partial(lax.dot_general, dimension_numbers=_NT, preferred_element_type=F32)
    return d(ah, bh) + d(ah, bl) + d(al, bh)


def _rms_rows(x, g):
    ms = jnp.mean(x * x, axis=-1, keepdims=True)
    return x * lax.rsqrt(ms + RMS_EPS) * g


def _group_mean_sq(t, gmat):
    t2 = t * t
    hi, lo = _split(t2)
    return (jnp.dot(hi, gmat, preferred_element_type=F32)
            + jnp.dot(lo, gmat, preferred_element_type=F32))


def _head_norm(t, gmat, gain):
    slab = gmat.shape[0]
    parts = [_group_mean_sq(t[:, s:s + slab], gmat) for s in range(0, t.shape[-1], slab)]
    ms = jnp.concatenate(parts, axis=1)
    return t * lax.rsqrt(ms + RMS_EPS) * gain


def _sigmoid(x):
    return 1.0 / (1.0 + jnp.exp(-x))


def _rel_bucket(d):
    dist = jnp.maximum(d, 0)
    exact = REL_BUCKETS // 2
    log_ratio = jnp.log(jnp.maximum(dist, 1).astype(F32) / exact) / math.log(REL_MAX_DIST / exact)
    large = exact + (log_ratio * (REL_BUCKETS - exact)).astype(jnp.int32)
    return jnp.where(dist < exact, dist, jnp.minimum(large, REL_BUCKETS - 1))


def _bias_lookup(bucket, tab_ref, h):
    out = jnp.zeros(bucket.shape, F32)
    for b in range(REL_BUCKETS):
        out = jnp.where(bucket == b, tab_ref[b, h], out)
    return out


def _bias_tiles_kernel(tab_ref, o_ref):
    h = pl.program_id(0)
    key = lax.broadcasted_iota(jnp.int32, (ATT_TILE, ATT_TILE), 0)
    qry = lax.broadcasted_iota(jnp.int32, (ATT_TILE, ATT_TILE), 1)
    d0 = qry - key
    far = tab_ref[REL_BUCKETS - 1, h]
    near0 = (_bias_lookup(_rel_bucket(d0), tab_ref, h) - far) * LOG2E
    o_ref[0, 0] = jnp.where(d0 >= 0, near0, -jnp.inf)
    o_ref[0, 1] = (_bias_lookup(_rel_bucket(d0 + ATT_TILE), tab_ref, h) - far) * LOG2E


def _bias_sample_kernel(tab_ref, o_ref, *, past_len, n_new):
    h = pl.program_id(0)
    shape = o_ref.shape[1:]
    i = lax.broadcasted_iota(jnp.int32, shape, 0)
    k = lax.broadcasted_iota(jnp.int32, shape, 1)
    d = past_len + i - k
    ok = (d >= 0) & (k < past_len + n_new)
    o_ref[0] = jnp.where(ok, _bias_lookup(_rel_bucket(d), tab_ref, h), -jnp.inf)


def _build_bias(rel_bias, past_len, n_new):
    n_heads = rel_bias.shape[1]
    smem = pl.BlockSpec(memory_space=pltpu.SMEM)
    tiles = pl.pallas_call(
        _bias_tiles_kernel,
        grid=(n_heads,),
        in_specs=[smem],
        out_specs=pl.BlockSpec((1, 2, ATT_TILE, ATT_TILE), lambda h: (h, 0, 0, 0)),
        out_shape=jax.ShapeDtypeStruct((n_heads, 2, ATT_TILE, ATT_TILE), F32),
        compiler_params=_cparams(("arbitrary",)),
        name="bias_tiles",
    )(rel_bias)
    width = past_len + PAGE_SIZE
    sample = pl.pallas_call(
        functools.partial(_bias_sample_kernel, past_len=past_len, n_new=n_new),
        grid=(n_heads,),
        in_specs=[smem],
        out_specs=pl.BlockSpec((1, n_new, width), lambda h: (h, 0, 0)),
        out_shape=jax.ShapeDtypeStruct((n_heads, n_new, width), F32),
        compiler_params=_cparams(("arbitrary",)),
        name="bias_sample",
    )(rel_bias)
    return tiles, sample.reshape(n_heads * n_new, width)


def _first_half_mask(shape):
    lane = lax.broadcasted_iota(jnp.int32, shape, 1)
    return (lane & HEAD_DIM) == 0


def _in_proj_kernel(x_ref, g_ref, w_ref, qg_ref, kg_ref, gm_ref, *outs, tm, width, has_u, prompt):
    h = _rms_rows(x_ref[...], g_ref[...]).astype(BF16)
    y = jnp.dot(h, w_ref[...], preferred_element_type=F32)
    gm = gm_ref[...]
    qn = _head_norm(y[:, 0:width], gm, qg_ref[...]) * (HEAD_DIM ** -0.5)
    kn = _head_norm(y[:, width:2 * width], gm, kg_ref[...])
    v = y[:, 2 * width:3 * width]
    if not prompt:
        outs[0][...] = qn
        outs[1][...] = kn
        outs[2][...] = v
        if has_u:
            outs[3][...] = y[:, 3 * width:4 * width]
        return
    qa_ref, qb_ref, kb_ref, kt_ref, v_ref, vt_ref = outs[:6]
    first = _first_half_mask(qn.shape)
    ql = qn * LOG2E
    qa_ref[...] = jnp.where(first, ql, 0.0).astype(BF16)
    qb_ref[...] = jnp.where(first, 0.0, ql).astype(BF16)
    kb_ref[...] = kn.astype(BF16)
    kt_ref[0] = kn.T
    v_t = v.T
    if has_u:
        v_ref[0] = v_t
    else:
        v_ref[...] = v
    v_tb = v_t.astype(BF16)
    ones = jnp.ones((VT_ROWS - 128, ATT_TILE), BF16)
    for g in range(width // 128):
        for j in range(tm // ATT_TILE):
            vt_ref[0, g, j] = jnp.concatenate(
                [v_tb[g * 128:(g + 1) * 128, j * ATT_TILE:(j + 1) * ATT_TILE], ones], axis=0)
    if has_u:
        u_ref, km_ref = outs[6:]
        u_ref[...] = y[:, 3 * width:4 * width]
        for j in range(tm // MOBA_BLOCK):
            km_ref[j] = jnp.mean(kn[j * MOBA_BLOCK:(j + 1) * MOBA_BLOCK], axis=0, keepdims=True)


NORM_SLAB = 256


def _group_matrix():
    i = jnp.arange(NORM_SLAB) // HEAD_DIM
    return jnp.where(i[:, None] == i[None, :], 1.0 / HEAD_DIM, 0.0).astype(BF16)


def _row_spec(tm, width):
    return pl.BlockSpec((tm, width), lambda i: (i, 0))


def _const_spec(shape):
    nd = len(shape)
    return pl.BlockSpec(shape, lambda *_: (0,) * nd)


def _in_proj(x, gain, w_bf, q_gain, k_gain, gmat, *, width, has_u, seq=None):
    n = x.shape[0]
    prompt = seq is not None
    tm = min(512, n)
    sds = jax.ShapeDtypeStruct
    n_out = 4 if has_u else 3
    if not prompt:
        out_shape = tuple(sds((n, width), F32) for _ in range(n_out))
        out_specs = tuple(_row_spec(tm, width) for _ in range(n_out))
    else:
        batch = n // seq
        tps = seq // tm
        groups = width // 128
        feat_spec = pl.BlockSpec((1, width, tm), lambda i: (i // tps, 0, i % tps))
        out_shape = [sds((n, width), BF16), sds((n, width), BF16), sds((n, width), BF16),
                     sds((batch, width, seq), F32),
                     sds((batch, width, seq), F32) if has_u else sds((n, width), F32),
                     sds((batch, groups, seq // ATT_TILE, VT_ROWS, ATT_TILE), BF16)]
        out_specs = [_row_spec(tm, width)] * 3 + [
            feat_spec, feat_spec if has_u else _row_spec(tm, width),
            pl.BlockSpec((1, groups, tm // ATT_TILE, VT_ROWS, ATT_TILE),
                         lambda i: (i // tps, 0, i % tps, 0, 0))]
        if has_u:
            out_shape += [sds((n, width), F32), sds((n // MOBA_BLOCK, 1, width), F32)]
            out_specs += [_row_spec(tm, width),
                          pl.BlockSpec((tm // MOBA_BLOCK, 1, width), lambda i: (i, 0, 0))]
    return pl.pallas_call(
        functools.partial(_in_proj_kernel, tm=tm, width=width, has_u=has_u, prompt=prompt),
        grid=(n // tm,),
        in_specs=[_row_spec(tm, D_MODEL), _const_spec((1, D_MODEL)), _const_spec((D_MODEL, n_out * width)),
                  _const_spec((1, width)), _const_spec((1, width)), _const_spec((NORM_SLAB, NORM_SLAB))],
        out_specs=tuple(out_specs),
        out_shape=tuple(out_shape),
        compiler_params=_cparams(("arbitrary",)),
        name="ab_in" if has_u else "c_in",
    )(x, gain, w_bf, q_gain, k_gain, gmat)


def _attn_prompt_kernel(*refs, mode, lam_init):
    if mode == "moba":
        qa_ref, qb_ref, k_ref, vt_ref, bias_ref, km_ref, o_ref, acc_ref, sel_ref = refs
    else:
        (qa_ref, qb_ref, k_ref, vt_ref, bias_ref, lq1_ref, lk1_ref, lq2_ref, lk2_ref,
         hg_ref, o_ref, acc_ref) = refs
    qi = pl.program_id(2)
    t = ATT_TILE
    q2 = jnp.concatenate([qa_ref[0], qb_ref[0]], axis=0)
    if mode == "moba":
        tile = lambda j: jnp.concatenate([bias_ref[0, j], bias_ref[1, j]], axis=1)
    else:
        tile = lambda j: jnp.concatenate([bias_ref[0, j], bias_ref[0, j]], axis=1)

    acc_ref[...] = jnp.zeros(acc_ref.shape, F32)

    if mode == "moba":
        km_hi, km_lo = _split(km_ref[0])
        gate = (lax.dot_general(km_hi, q2, _NT, preferred_element_type=F32)
                + lax.dot_general(km_lo, q2, _NT, preferred_element_type=F32))
        blk_i = lax.broadcasted_iota(jnp.int32, gate.shape, 0)
        blk_f = blk_i.astype(F32)
        valid = blk_i < qi
        cur = jnp.where(valid, gate, -jnp.inf)
        sel_pen = jnp.full(gate.shape, -jnp.inf, F32)
        for _ in range(MOBA_TOPK):
            top = jnp.max(cur, axis=0, keepdims=True)
            idx = jnp.min(jnp.where(cur == top, blk_f, 1e9), axis=0, keepdims=True)
            pick = blk_f == idx
            sel_pen = jnp.where(pick, jnp.where(valid, 0.0, -jnp.inf), sel_pen)
            cur = jnp.where(pick, -jnp.inf, cur)
        sel_ref[...] = sel_pen
        block_pen = lambda kj: sel_ref[pl.ds(kj, 1), :]
    else:
        block_pen = None

    def tiles(m_prev, kjs, bias_adds):
        scores = []
        for kj, bias_add in zip(kjs, bias_adds):
            start = pl.multiple_of(kj * t, t)
            s = lax.dot_general(k_ref[0, pl.ds(start, t), :], q2, _NT,
                                preferred_element_type=F32)
            scores.append(s if bias_add is None else s + bias_add)
        parts = []
        for s in scores:
            m_t = jnp.maximum(jnp.max(s, axis=0, keepdims=True), M_INIT)
            parts.append((m_t, jnp.exp2(s - m_t).astype(BF16)))
        m_new = m_prev
        for m_t, _ in parts:
            m_new = jnp.maximum(m_new, m_t)
        acc = jnp.exp2(m_prev - m_new) * acc_ref[...]
        for kj, (m_t, p) in zip(kjs, parts):
            acc = acc + jnp.exp2(m_t - m_new) * jnp.dot(vt_ref[0, 0, kj], p,
                                                        preferred_element_type=F32)
        acc_ref[...] = acc
        return m_new

    def far_pen(kj):
        return None if block_pen is None else block_pen(kj)

    def far_run(first, count, m_prev):
        kjs = [first + j for j in range(count)]
        return tiles(m_prev, kjs, [far_pen(kj) for kj in kjs])

    n_far = jnp.maximum(qi - 1, 0)
    m = jnp.full((1, 2 * t), M_INIT, F32)
    n_quads = lax.shift_right_logical(n_far, 2)
    m = lax.fori_loop(0, n_quads, lambda i, c: far_run(4 * i, 4, c), m)

    def last_group(n_left):
        def run(m_prev):
            kn = jnp.maximum(qi - 1, 0)
            near = tile(1) + jnp.where(qi >= 1, 0.0, -jnp.inf)
            if block_pen is not None:
                near = near + block_pen(kn)
            left = [4 * n_quads + j for j in range(n_left)]
            return tiles(m_prev, left + [kn, qi], [far_pen(kj) for kj in left] + [near, tile(0)])
        return run

    lax.switch(n_far & 3, [last_group(n) for n in range(4)], m)

    out = acc_ref[0:128, :] / acc_ref[128:129, :]
    if mode == "moba":
        row = lax.broadcasted_iota(jnp.int32, (128, t), 0)
        o_ref[0] = jnp.where(row < HEAD_DIM, out[:, :t], out[:, t:]).T
    else:
        l1 = jnp.sum(lq1_ref[...] * lk1_ref[...], axis=1, keepdims=True)
        l2 = jnp.sum(lq2_ref[...] * lk2_ref[...], axis=1, keepdims=True)
        lam = jnp.exp(l1) - jnp.exp(l2) + lam_init
        o = (out[:, :t] - lam * out[:, t:]).T
        o_ref[0] = _rms_rows(o, hg_ref[...]) * (1.0 - lam_init)


def _attn_prompt(mode, qa, qb, kb, vt, bias_tiles, extra, *, batch, lam_init=0.0):
    _, seq, width = qa.shape
    groups = width // 128
    nq = seq // ATT_TILE
    q_spec = pl.BlockSpec((1, ATT_TILE, 128), lambda b, g, i: (b, i, g))
    k_spec = pl.BlockSpec((1, seq, 128), lambda b, g, i: (b, 0, g))
    vt_spec = pl.BlockSpec((1, 1, nq, VT_ROWS, ATT_TILE), lambda b, g, i: (b, g, 0, 0, 0))
    scratch = [pltpu.VMEM((VT_ROWS, 2 * ATT_TILE), F32)]
    if mode == "moba":
        (kmean,) = extra
        nb = kmean.shape[1]
        assert nb % 8 == 0
        bias_spec = pl.BlockSpec((2, 2, ATT_TILE, ATT_TILE), lambda b, g, i: (g, 0, 0, 0))
        extra_specs = [pl.BlockSpec((1, nb, 128), lambda b, g, i: (b, 0, g))]
        scratch.append(pltpu.VMEM((nb, 2 * ATT_TILE), F32))
    else:
        bias_spec = pl.BlockSpec((1, 2, ATT_TILE, ATT_TILE), lambda b, g, i: (g, 0, 0, 0))
        extra_specs = [_const_spec((1, HEAD_DIM))] * 4 + [_const_spec((1, 128))]
    return pl.pallas_call(
        functools.partial(_attn_prompt_kernel, mode=mode, lam_init=lam_init),
        grid=(batch, groups, nq),
        in_specs=[q_spec, q_spec, k_spec, vt_spec, bias_spec] + extra_specs,
        out_specs=pl.BlockSpec((1, ATT_TILE, 128), lambda b, g, i: (b, i, g)),
        out_shape=jax.ShapeDtypeStruct((batch, seq, width), F32),
        scratch_shapes=scratch,
        compiler_params=_cparams(("arbitrary", "arbitrary", "arbitrary")),
        name=f"{mode}_prompt",
    )(qa, qb, kb, vt, bias_tiles, *extra)


def _page_map(b, pt_ref, *, j, li):
    return (pt_ref[b, j], li, 0, 0)


def _pad_rows(x, rows):
    return jnp.concatenate([x, jnp.zeros((rows - x.shape[0], x.shape[1]), x.dtype)], axis=0)


def _softmax_pv(scores, values, transposed):
    mx = scores[0]
    for s in scores[1:]:
        mx = jnp.maximum(mx, s)
    m = jnp.max(mx, axis=1, keepdims=True)
    tot = None
    acc = None
    for s, v, tr in zip(scores, values, transposed):
        p = jnp.exp(s - m)
        tot = p if tot is None else tot + p
        if tr:
            d = lax.dot_general(p.astype(BF16), v, _NT, preferred_element_type=F32)
        else:
            d = jnp.dot(p.astype(BF16), v, preferred_element_type=F32)
        acc = d if acc is None else acc + d
    return acc / jnp.sum(tot, axis=1, keepdims=True)


def _moba_sample_kernel(pt_ref, q_ref, kn_ref, vn_ref, bias_ref, *rest, n_pages):
    kp = rest[:n_pages]
    vp = rest[n_pages:2 * n_pages]
    o_ref = rest[2 * n_pages]
    nq = q_ref.shape[1]
    heads = MOBA_WIDTH // HEAD_DIM
    rows = heads * nq
    q = q_ref[0]
    q_rep = jnp.concatenate([q] * heads, axis=0)
    row = lax.broadcasted_iota(jnp.int32, (rows, MOBA_WIDTH), 0)
    lane = lax.broadcasted_iota(jnp.int32, (rows, MOBA_WIDTH), 1)
    own_head = (row >> _log2(nq)) == (lane >> _log2(HEAD_DIM))
    qbd = jnp.where(own_head, q_rep, 0.0).astype(BF16)

    pages_per_block = MOBA_BLOCK // PAGE_SIZE
    n_blocks = n_pages // pages_per_block
    scores = [jnp.dot(qbd, kp[j][0, 0].astype(BF16), preferred_element_type=F32)
              for j in range(n_pages)]
    lane_i = lax.broadcasted_iota(jnp.int32, (rows, 128), 1)
    gate = jnp.zeros((rows, 128), F32)
    for n in range(n_blocks):
        blk = scores[n * pages_per_block]
        for j in range(n * pages_per_block + 1, (n + 1) * pages_per_block):
            blk = blk + scores[j]
        gate = jnp.where(lane_i == n, jnp.sum(blk, axis=1, keepdims=True) * (1.0 / MOBA_BLOCK), gate)
    lane_f = lane_i.astype(F32)
    valid = lane_i < n_blocks
    cur = jnp.where(valid, gate, -jnp.inf)
    sel_pen = jnp.full(gate.shape, -jnp.inf, F32)
    for _ in range(MOBA_TOPK):
        top = jnp.max(cur, axis=1, keepdims=True)
        idx = jnp.min(jnp.where(cur == top, lane_f, 1e9), axis=1, keepdims=True)
        pick = lane_f == idx
        sel_pen = jnp.where(pick, jnp.where(valid, 0.0, -jnp.inf), sel_pen)
        cur = jnp.where(pick, -jnp.inf, cur)

    for j in range(n_pages):
        n = j // pages_per_block
        scores[j] = scores[j] + bias_ref[:, j * PAGE_SIZE:(j + 1) * PAGE_SIZE] + sel_pen[:, n:n + 1]
    values = [vp[j][0, 0].astype(BF16) for j in range(n_pages)]
    k_new = _pad_rows(kn_ref[0], PAGE_SIZE).astype(BF16)
    scores.append(lax.dot_general(qbd, k_new, _NT, preferred_element_type=F32)
                  + bias_ref[:, n_pages * PAGE_SIZE:(n_pages + 1) * PAGE_SIZE])
    values.append(_pad_rows(vn_ref[0], PAGE_SIZE).astype(BF16))

    res = _softmax_pv(scores, values, [True] * n_pages + [False])
    res = jnp.where(own_head, res, 0.0)
    out = res[0:nq]
    for h in range(1, heads):
        out = out + res[h * nq:(h + 1) * nq]
    o_ref[0] = out


def _diff_sample_kernel(pt_ref, q_ref, kn_ref, vn_ref, bias_ref, lq1_ref, lk1_ref, lq2_ref, lk2_ref,
                        hg_ref, *rest, n_pages, lam_init):
    kp = rest[:n_pages]
    vp = rest[n_pages:2 * n_pages]
    o_ref = rest[2 * n_pages]
    nq = q_ref.shape[1]
    heads = DIFF_HEADS
    hw = 2 * HEAD_DIM
    l1 = jnp.sum(lq1_ref[...] * lk1_ref[...], axis=1, keepdims=True)
    l2 = jnp.sum(lq2_ref[...] * lk2_ref[...], axis=1, keepdims=True)
    lam = jnp.exp(l1) - jnp.exp(l2) + lam_init
    lane = lax.broadcasted_iota(jnp.int32, (nq, hw), 1)
    new_cols = slice(n_pages * PAGE_SIZE, (n_pages + 1) * PAGE_SIZE)
    for h in range(heads):
        cols = slice(h * hw, (h + 1) * hw)
        qh = q_ref[0, :, cols]
        q2 = jnp.concatenate([jnp.where(lane < HEAD_DIM, qh, 0.0),
                              jnp.where(lane < HEAD_DIM, 0.0, qh)], axis=0).astype(BF16)
        bias = bias_ref[2 * nq * h:2 * nq * (h + 1), :]
        scores = []
        values = []
        for j in range(n_pages):
            kj = kp[j][0, 0, cols, :].astype(BF16)
            scores.append(jnp.dot(q2, kj, preferred_element_type=F32)
                          + bias[:, j * PAGE_SIZE:(j + 1) * PAGE_SIZE])
            values.append(vp[j][0, 0, pl.ds(h, PAGE_SIZE, stride=heads), :].astype(BF16))
        k_new = _pad_rows(kn_ref[0, :, cols], PAGE_SIZE).astype(BF16)
        scores.append(lax.dot_general(q2, k_new, _NT, preferred_element_type=F32) + bias[:, new_cols])
        values.append(_pad_rows(vn_ref[0, :, cols], PAGE_SIZE).astype(BF16))
        res = _softmax_pv(scores, values, [False] * (n_pages + 1))
        d = res[:nq] - lam * res[nq:]
        o_ref[0, :, cols] = _rms_rows(d, hg_ref[...]) * (1.0 - lam_init)


def _sample_attn(mode, page_table, q, k_new, v_new, bias_rows, cache_k, cache_v, li, extra=(),
                 lam_init=0.0):
    batch, n_new, width = q.shape
    n_pages = page_table.shape[1]
    tok_spec = pl.BlockSpec((1, n_new, width), lambda b, pt: (b, 0, 0))
    const = lambda shape: pl.BlockSpec(shape, lambda b, pt: (0,) * len(shape))
    page_specs = [pl.BlockSpec((1, 1, width, PAGE_SIZE), functools.partial(_page_map, j=j, li=li))
                  for j in range(n_pages)]
    if mode == "moba":
        body = functools.partial(_moba_sample_kernel, n_pages=n_pages)
        extra_specs = []
    else:
        body = functools.partial(_diff_sample_kernel, n_pages=n_pages, lam_init=lam_init)
        extra_specs = [const((1, HEAD_DIM))] * 4 + [const((1, 2 * HEAD_DIM))]
    grid_spec = pltpu.PrefetchScalarGridSpec(
        num_scalar_prefetch=1,
        grid=(batch,),
        in_specs=[tok_spec, tok_spec, tok_spec, const(bias_rows.shape)] + extra_specs
                 + page_specs + page_specs,
        out_specs=tok_spec,
    )
    return pl.pallas_call(
        body,
        grid_spec=grid_spec,
        out_shape=jax.ShapeDtypeStruct((batch, n_new, width), F32),
        compiler_params=_cparams(("arbitrary",)),
        name=f"{mode}_sample",
    )(page_table, q, k_new, v_new, bias_rows, *extra, *([cache_k] * n_pages), *([cache_v] * n_pages))


def _s5_tables_kernel(are_ref, aim_ref, ldt_ref, btre_ref, btim_ref, cre_ref, cim_ref,
                      xre_ref, xim_ref, zre_ref, zim_ref, wre_ref, wim_ref, vre_ref, vim_ref,
                      lre_ref, lim_ref, *, chunk):
    a_re = are_ref[...]
    a_im = aim_ref[...]
    dt = jnp.exp(ldt_ref[...])
    mag = jnp.exp(a_re * dt)
    lb_re = mag * jnp.cos(a_im * dt)
    lb_im = mag * jnp.sin(a_im * dt)
    den = a_re * a_re + a_im * a_im
    f_re = ((lb_re - 1.0) * a_re + lb_im * a_im) / den
    f_im = (lb_im * a_re - (lb_re - 1.0) * a_im) / den
    bb_re = f_re * btre_ref[...] - f_im * btim_ref[...]
    bb_im = f_re * btim_ref[...] + f_im * btre_ref[...]
    c_re = cre_ref[...]
    c_im = cim_ref[...]
    inv_mag = jnp.exp(-(a_re * dt))
    il_re = inv_mag * jnp.cos(a_im * dt)
    il_im = -inv_mag * jnp.sin(a_im * dt)

    def cmul(x_re, x_im, y_re, y_im):
        return x_re * y_re - x_im * y_im, x_re * y_im + x_im * y_re

    def put(ref, j, val):
        ref[:, j * S5_GROUP:(j + 1) * S5_GROUP, :] = val.reshape(S5_GROUPS, S5_GROUP, S5_STATE)

    one = jnp.ones_like(a_re)
    zero = jnp.zeros_like(a_re)
    pos = [(one, zero)]
    neg = [(one, zero)]
    for _ in range(chunk):
        pos.append(cmul(pos[-1][0], pos[-1][1], lb_re, lb_im))
        neg.append(cmul(neg[-1][0], neg[-1][1], il_re, il_im))
    for j in range(chunk):
        x = cmul(bb_re, bb_im, *neg[j])
        z = cmul(c_re, c_im, *pos[j])
        w = cmul(bb_re, bb_im, *pos[chunk - 1 - j])
        v = cmul(c_re, c_im, *pos[j + 1])
        put(xre_ref, j, x[0]); put(xim_ref, j, x[1])
        put(zre_ref, j, z[0]); put(zim_ref, j, z[1])
        put(wre_ref, j, w[0]); put(wim_ref, j, w[1])
        put(vre_ref, j, v[0]); put(vim_ref, j, -v[1])
    lre_ref[...] = pos[chunk][0].reshape(S5_GROUPS, S5_GROUP, S5_STATE)[:, 0:1, :]
    lim_ref[...] = pos[chunk][1].reshape(S5_GROUPS, S5_GROUP, S5_STATE)[:, 0:1, :]


def _s5_toeplitz_kernel(xre_ref, xim_ref, zre_ref, zim_ref, m_ref):
    m = _dot3_nt(xre_ref[0], zre_ref[0]) - _dot3_nt(xim_ref[0], zim_ref[0])
    r = lax.broadcasted_iota(jnp.int32, m.shape, 0) >> _log2(S5_GROUP)
    c = lax.broadcasted_iota(jnp.int32, m.shape, 1) >> _log2(S5_GROUP)
    m_ref[0] = jnp.where(c >= r, m, 0.0)


def _s5_tables(a_re, a_im, log_dt, b_re, b_im, c_re, c_im, d_skip, chunk):
    g, n, p = S5_GROUPS, S5_STATE, S5_GROUP
    kl = chunk * p
    rep = lambda x: jnp.repeat(x, p, axis=0)
    flat = lambda x: x.reshape(g * p, n)
    ins = (rep(a_re), rep(a_im), rep(jnp.broadcast_to(log_dt[:, None], (g, n))),
           flat(b_re.transpose(0, 2, 1)), flat(b_im.transpose(0, 2, 1)), flat(c_re), flat(c_im))
    big = jax.ShapeDtypeStruct((g, kl, n), F32)
    small = jax.ShapeDtypeStruct((g, 1, n), F32)
    outs = pl.pallas_call(
        functools.partial(_s5_tables_kernel, chunk=chunk),
        out_shape=(big,) * 8 + (small, small),
        compiler_params=pltpu.CompilerParams(vmem_limit_bytes=VMEM_LIMIT),
        name="s5_tables",
    )(*ins)
    x_re, x_im, z_re, z_im, w_re, w_im, v_re, v_im, l_re, l_im = outs
    gspec = pl.BlockSpec((1, kl, n), lambda i: (i, 0, 0))
    toep = pl.pallas_call(
        _s5_toeplitz_kernel,
        grid=(g,),
        in_specs=[gspec] * 4,
        out_specs=pl.BlockSpec((1, kl, kl), lambda i: (i, 0, 0)),
        out_shape=jax.ShapeDtypeStruct((g, kl, kl), F32),
        compiler_params=_cparams(("arbitrary",)),
        name="s5_toeplitz",
    )(x_re, x_im, z_re, z_im)

    def pad_lo(x):
        return jnp.pad(x, [(0, 0)] * (x.ndim - 1) + [(0, n)])

    def pad_hi(x):
        return jnp.pad(x, [(0, 0)] * (x.ndim - 1) + [(n, 0)])

    def pair_rows(w):
        w = w.reshape(g // 2, 2, kl, n)
        return jnp.concatenate([pad_lo(w[:, 0]), pad_hi(w[:, 1])], axis=1)

    def pair_pad(v):
        v = v.reshape(g // 2, 2, kl, n)
        return jnp.stack([pad_lo(v[:, 0]), pad_hi(v[:, 1])], axis=1).reshape(g, kl, 2 * n)

    def pair_lanes(l):
        l = l.reshape(g // 2, 2, 1, n)
        return jnp.concatenate([l[:, 0], l[:, 1]], axis=-1)

    d_tile = jnp.tile(d_skip, (1, chunk)).reshape(g, 1, kl)
    return dict(chunk=chunk, toep=toep, w_re=pair_rows(w_re), w_im=pair_rows(w_im),
                v_re=pair_pad(v_re), v_im=pair_pad(v_im), d=d_tile,
                l_re=pair_lanes(l_re), l_im=pair_lanes(l_im))


S5_GROUPS_PER_STEP = 4


def _s5_apply_kernel(u_ref, m_ref, wre_ref, wim_ref, vre_ref, vim_ref, d_ref, lre_ref, lim_ref,
                     s0re_ref, s0im_ref, y_ref, stre_ref, stim_ref, ere_ref, eim_ref, sre_ref, sim_ref,
                     *, n_chunks, batch):
    pairs = S5_GROUPS_PER_STEP // 2
    for k in range(pairs):
        ucat = jnp.concatenate([u_ref[2 * k], u_ref[2 * k + 1]], axis=1)
        ere_ref[k] = _dot3(ucat, wre_ref[k])
        eim_ref[k] = _dot3(ucat, wim_ref[k])
    l_re = lre_ref[...]
    l_im = lim_ref[...]

    def step(c, carry):
        s_re, s_im = carry
        rows = pl.ds(c * batch, batch)
        sre_ref[:, rows, :] = s_re
        sim_ref[:, rows, :] = s_im
        n_re = l_re * s_re - l_im * s_im + ere_ref[:, rows, :]
        n_im = l_re * s_im + l_im * s_re + eim_ref[:, rows, :]
        return n_re, n_im

    s_re, s_im = lax.fori_loop(0, n_chunks, step, (s0re_ref[...], s0im_ref[...]))
    stre_ref[...] = s_re
    stim_ref[...] = s_im
    for g in range(S5_GROUPS_PER_STEP):
        u = u_ref[g]
        y_ref[g] = (_dot3(u, m_ref[g]) + _dot3_nt(sre_ref[g // 2], vre_ref[g])
                    + _dot3_nt(sim_ref[g // 2], vim_ref[g]) + d_ref[g] * u)


def _s5_apply(u_chunks, tabs, s0_re, s0_im, *, n_chunks, batch):
    g, rows, kl = u_chunks.shape
    gs = S5_GROUPS_PER_STEP
    ps = gs // 2
    gspec = lambda shape: pl.BlockSpec((gs,) + shape, lambda i: (i, 0, 0))
    pspec = lambda shape: pl.BlockSpec((ps,) + shape, lambda i: (i, 0, 0))
    return pl.pallas_call(
        functools.partial(_s5_apply_kernel, n_chunks=n_chunks, batch=batch),
        grid=(g // gs,),
        in_specs=[gspec((rows, kl)), gspec((kl, kl)), pspec((2 * kl, 128)), pspec((2 * kl, 128)),
                  gspec((kl, 128)), gspec((kl, 128)), gspec((1, kl)), pspec((1, 128)), pspec((1, 128)),
                  pspec((batch, 128)), pspec((batch, 128))],
        out_specs=(gspec((rows, kl)), pspec((batch, 128)), pspec((batch, 128))),
        out_shape=(jax.ShapeDtypeStruct((g, rows, kl), F32),
                   jax.ShapeDtypeStruct((g // 2, batch, 128), F32),
                   jax.ShapeDtypeStruct((g // 2, batch, 128), F32)),
        scratch_shapes=[pltpu.VMEM((ps, rows, 128), F32)] * 4,
        compiler_params=_cparams(("arbitrary",)),
        name="s5_apply",
    )(u_chunks, tabs["toep"], tabs["w_re"], tabs["w_im"], tabs["v_re"], tabs["v_im"], tabs["d"],
      tabs["l_re"], tabs["l_im"], s0_re, s0_im)


def _s5(u, tabs, s0_re, s0_im, *, batch, seq):
    chunk = tabs["chunk"]
    g, p, n = S5_GROUPS, S5_GROUP, S5_STATE
    n_chunks = seq // chunk
    uc = u.reshape(batch, n_chunks, chunk, g, p).transpose(3, 1, 0, 2, 4)
    uc = uc.reshape(g, n_chunks * batch, chunk * p)
    pack = lambda s: s.reshape(batch, g // 2, 2 * n).transpose(1, 0, 2)
    y, st_re, st_im = _s5_apply(uc, tabs, pack(s0_re), pack(s0_im), n_chunks=n_chunks, batch=batch)
    y = y.reshape(g, n_chunks, batch, chunk, p).transpose(2, 1, 3, 0, 4).reshape(batch * seq, g * p)
    unpack = lambda s: s.transpose(1, 0, 2).reshape(batch, g, n)
    return y, unpack(st_re), unpack(st_im)


def _gelu_tanh(x):
    return 0.5 * x * (1.0 + jnp.tanh(math.sqrt(2.0 / math.pi) * (x + 0.044715 * (x * x * x))))


def _ab_out_kernel(x_ref, oa_ref, y_ref, wglu_ref, w_ref, o_ref):
    gl = _gelu_tanh(y_ref[...])
    gate = jnp.dot(gl.astype(BF16), wglu_ref[...], preferred_element_type=F32)
    ob = gl * _sigmoid(gate)
    cat = jnp.concatenate([oa_ref[...].astype(BF16), ob.astype(BF16)], axis=1)
    o_ref[...] = x_ref[...] + jnp.dot(cat, w_ref[...], preferred_element_type=F32)


def _c_out_kernel(x_ref, o_in_ref, w_ref, o_ref):
    o_ref[...] = x_ref[...] + jnp.dot(o_in_ref[...].astype(BF16), w_ref[...],
                                      preferred_element_type=F32)


def _ab_out(x, o_a, y, wglu_bf, w_bf):
    n = x.shape[0]
    tm = min(512, n)
    return pl.pallas_call(
        _ab_out_kernel,
        grid=(n // tm,),
        in_specs=[_row_spec(tm, D_MODEL), _row_spec(tm, 512), _row_spec(tm, 512),
                  _const_spec((512, 512)), _const_spec((D_MODEL, D_MODEL))],
        out_specs=_row_spec(tm, D_MODEL),
        out_shape=jax.ShapeDtypeStruct((n, D_MODEL), F32),
        compiler_params=_cparams(("arbitrary",)),
        name="ab_out",
    )(x, o_a, y, wglu_bf, w_bf)


def _c_out(x, o, w_bf):
    n = x.shape[0]
    tm = min(512, n)
    return pl.pallas_call(
        _c_out_kernel,
        grid=(n // tm,),
        in_specs=[_row_spec(tm, D_MODEL), _row_spec(tm, D_MODEL), _const_spec((D_MODEL, D_MODEL))],
        out_specs=_row_spec(tm, D_MODEL),
        out_shape=jax.ShapeDtypeStruct((n, D_MODEL), F32),
        compiler_params=_cparams(("arbitrary",)),
        name="c_out",
    )(x, o, w_bf)


FF_SUB_BLOCKS = 4


def _ffn_kernel(*refs, tm, carry, period):
    if carry:
        x_ref, g_ref, wg_ref, wu_ref, cw_ref, wdn_ref, y_ref, cs_ref, hb_ref, acc_ref, halo_ref = refs
    else:
        (x_ref, g_ref, wg_ref, wu_ref, cw_ref, wdn_ref, old0_ref, old1_ref, y_ref, gout_ref,
         hb_ref, acc_ref) = refs
    i = pl.program_id(1)
    c = pl.program_id(2)

    @pl.when(c == 0)
    def _():
        x = x_ref[0]
        hb_ref[...] = _rms_rows(x, g_ref[...]).astype(BF16)
        acc_ref[...] = x

    sub = tm // FF_SUB_BLOCKS
    assert sub % 8 == 0 and (carry or sub % period == 0)
    blocks = [slice(s * sub, (s + 1) * sub) for s in range(FF_SUB_BLOCKS)]
    gates = [jnp.dot(hb_ref[r, :], wg_ref[...], preferred_element_type=F32) for r in blocks]
    ups = [jnp.dot(hb_ref[r, :], wu_ref[...], preferred_element_type=F32) for r in blocks]
    cw = cw_ref[...]
    row = lax.broadcasted_iota(jnp.int32, (sub, FF_CHUNK), 0)
    if carry:
        prev = jnp.where(i == 0, 0.0, halo_ref[c])
    for s, r in enumerate(blocks):
        gate = gates[s]
        r1 = pltpu.roll(gate, 1, 0)
        r2 = pltpu.roll(gate, 2, 0)
        if carry:
            p1 = jnp.where(row == 0, prev[7:8], r1)
            p2 = jnp.where(row == 0, prev[6:7], jnp.where(row == 1, prev[7:8], r2))
            prev = gate[sub - 8:sub]
        else:
            t = row & (period - 1)
            nseq = sub // period
            rep = lambda b: jnp.broadcast_to(b[:, None, :], (nseq, period, FF_CHUNK)).reshape(sub, FF_CHUNK)
            old0 = rep(old0_ref[s * nseq:(s + 1) * nseq, :])
            old1 = rep(old1_ref[s * nseq:(s + 1) * nseq, :])
            p1 = jnp.where(t == 0, old1, r1)
            p2 = jnp.where(t == 0, old0, jnp.where(t == 1, old1, r2))
            gout_ref[0, r, :] = gate
        conv = cw[3:4] + cw[0:1] * p2 + cw[1:2] * p1 + cw[2:3] * gate
        act = (conv * _sigmoid(conv)) * ups[s]
        acc_ref[r, :] += jnp.dot(act.astype(BF16), wdn_ref[...], preferred_element_type=F32)
    if carry:
        halo_ref[c] = prev
        cs_ref[0, 0] = prev

    @pl.when(c == N_FF_CHUNKS - 1)
    def _():
        y_ref[0] = acc_ref[...]


def _ffn_weights(gain, w_up, conv_w, conv_b, w_down):
    cw = jnp.concatenate([conv_w, conv_b[None], jnp.zeros((4, D_FF), F32)], axis=0)
    return gain.reshape(1, D_MODEL), w_up.astype(BF16), cw, w_down.astype(BF16)


def _ffn_common_specs():
    return [pl.BlockSpec((1, D_MODEL), lambda b, i, c: (0, 0)),
            pl.BlockSpec((D_MODEL, FF_CHUNK), lambda b, i, c: (0, c)),
            pl.BlockSpec((D_MODEL, FF_CHUNK), lambda b, i, c: (0, N_FF_CHUNKS + c)),
            pl.BlockSpec((8, FF_CHUNK), lambda b, i, c: (0, c)),
            pl.BlockSpec((FF_CHUNK, D_MODEL), lambda b, i, c: (c, 0))]


def _ffn_prompt(x, weights, *, batch, seq):
    tm = min(1024, seq)
    x_spec = pl.BlockSpec((1, tm, D_MODEL), lambda b, i, c: (b, i, 0))
    y, cs = pl.pallas_call(
        functools.partial(_ffn_kernel, tm=tm, carry=True, period=0),
        grid=(batch, seq // tm, N_FF_CHUNKS),
        in_specs=[x_spec] + _ffn_common_specs(),
        out_specs=(x_spec, pl.BlockSpec((1, 1, 8, FF_CHUNK), lambda b, i, c: (b * (seq // tm) + i, c, 0, 0))),
        out_shape=(jax.ShapeDtypeStruct((batch, seq, D_MODEL), F32),
                   jax.ShapeDtypeStruct((batch * (seq // tm), N_FF_CHUNKS, 8, FF_CHUNK), F32)),
        scratch_shapes=[pltpu.VMEM((tm, D_MODEL), BF16), pltpu.VMEM((tm, D_MODEL), F32),
                        pltpu.VMEM((N_FF_CHUNKS, 8, FF_CHUNK), F32)],
        compiler_params=_cparams(("arbitrary", "arbitrary", "arbitrary")),
        name="ffn_prompt",
    )(x, weights[0], weights[1], *weights[1:])
    cs = cs.reshape(batch, seq // tm, N_FF_CHUNKS, 8, FF_CHUNK)[:, -1]
    conv_state = cs[:, :, 6:8, :].transpose(0, 2, 1, 3).reshape(batch, 2, D_FF)
    return y, conv_state


def _ffn_sample(x, conv_buf, weights, *, batch, seq):
    n = batch * seq
    x_spec = pl.BlockSpec((1, n, D_MODEL), lambda b, i, c: (0, 0, 0))
    col_spec = pl.BlockSpec((1, n, FF_CHUNK), lambda b, i, c: (0, 0, c))
    buf_spec = pl.BlockSpec((batch, FF_CHUNK), lambda b, i, c: (0, c))
    y, gate = pl.pallas_call(
        functools.partial(_ffn_kernel, tm=n, carry=False, period=seq),
        grid=(1, 1, N_FF_CHUNKS),
        in_specs=[x_spec] + _ffn_common_specs() + [buf_spec, buf_spec],
        out_specs=(x_spec, col_spec),
        out_shape=(jax.ShapeDtypeStruct((1, n, D_MODEL), F32), jax.ShapeDtypeStruct((1, n, D_FF), F32)),
        scratch_shapes=[pltpu.VMEM((n, D_MODEL), BF16), pltpu.VMEM((n, D_MODEL), F32)],
        compiler_params=_cparams(("arbitrary", "arbitrary", "arbitrary")),
        name="ffn_sample",
    )(x[None], weights[0], weights[1], *weights[1:], conv_buf[:, 0], conv_buf[:, 1])
    return y[0], gate.reshape(batch, seq, D_FF)[:, seq - 2:]


def kernel(x_prompt, x_sample, cache_moba_k, cache_moba_v, state_s5_re, state_s5_im, cache_diff_k, cache_diff_v, state_ffn_conv, page_table, rel_bias, ab_norm, w_ab_in, w_ab_out, moba_q_gain, moba_k_gain, s5_a_re, s5_a_im, s5_log_dt, s5_b_re, s5_b_im, s5_c_re, s5_c_im, s5_d, s5_w_glu, c_norm, w_c_in, w_c_out, diff_q_gain, diff_k_gain, diff_lq1, diff_lk1, diff_lq2, diff_lk2, diff_head_gain, ffn_norm, w_ffn_up, ffn_conv_w, ffn_conv_b, w_ffn_down):
    batch, seq, _ = x_prompt.shape
    dbatch, dseq, _ = x_sample.shape
    depth = ffn_norm.shape[0]
    n_pages = page_table.shape[1]
    past_len = n_pages * PAGE_SIZE
    n_phys = cache_moba_k.shape[0]

    bias_tiles, bias_rows = _build_bias(rel_bias, past_len, dseq)
    gmat = _group_matrix()
    moba_k_pages = cache_moba_k.transpose(0, 1, 3, 4, 2).reshape(n_phys, -1, MOBA_WIDTH, PAGE_SIZE)
    moba_v_pages = cache_moba_v.transpose(0, 1, 3, 4, 2).reshape(n_phys, -1, MOBA_WIDTH, PAGE_SIZE)
    diff_k_pages = cache_diff_k.transpose(0, 1, 3, 4, 5, 2).reshape(n_phys, -1, D_MODEL, PAGE_SIZE)
    diff_v_pages = cache_diff_v.reshape(n_phys, -1, PAGE_SIZE * DIFF_HEADS, 2 * HEAD_DIM)
    diff_bias_rows = jnp.broadcast_to(
        bias_rows.reshape(DIFF_HEADS, 1, dseq, -1), (DIFF_HEADS, 2, dseq, bias_rows.shape[-1])
    ).reshape(2 * DIFF_HEADS * dseq, -1)
    zero_state = jnp.zeros((batch, S5_GROUPS, S5_STATE), F32)

    xp = x_prompt.reshape(batch * seq, D_MODEL)
    xs = x_sample.reshape(dbatch * dseq, D_MODEL)
    mk_p, mk_s, mv_p, mv_s = [], [], [], []
    sr_p, sr_s, si_p, si_s = [], [], [], []
    dk_p, dk_s, dv_p, dv_s = [], [], [], []
    cb_p, cb_s = [], []
    row = lambda v: v.reshape(1, -1)
    for layer in range(depth):
        li = layer // 2
        if layer % 2 == 0:
            gain = row(ab_norm[li])
            w_in = w_ab_in[li].astype(BF16)
            w_out = w_ab_out[li].astype(BF16)
            w_glu = s5_w_glu[li].astype(BF16)
            q_gain = row(jnp.tile(moba_q_gain[li], MOBA_WIDTH // HEAD_DIM))
            k_gain = row(jnp.tile(moba_k_gain[li], MOBA_WIDTH // HEAD_DIM))
            s5_args = (s5_a_re[li], s5_a_im[li], s5_log_dt[li], s5_b_re[li], s5_b_im[li],
                       s5_c_re[li], s5_c_im[li], s5_d[li])
            qa, qb, kb, kt, v_t, vt, u, kmean = _in_proj(xp, gain, w_in, q_gain, k_gain, gmat,
                                                         width=MOBA_WIDTH, has_u=True, seq=seq)
            shp = lambda a: a.reshape(batch, seq, MOBA_WIDTH)
            o_a = _attn_prompt("moba", shp(qa), shp(qb), shp(kb), vt, bias_tiles,
                               (kmean.reshape(batch, seq // MOBA_BLOCK, MOBA_WIDTH),), batch=batch)
            tabs = _s5_tables(*s5_args, chunk=16)
            y, s_re, s_im = _s5(u, tabs, zero_state, zero_state, batch=batch, seq=seq)
            xp = _ab_out(xp, o_a.reshape(batch * seq, MOBA_WIDTH), y, w_glu, w_out)
            rows_of = lambda a: a.reshape(batch, 8, HEAD_DIM, seq).transpose(0, 3, 1, 2)
            mk_p.append(rows_of(kt)); mv_p.append(rows_of(v_t))
            sr_p.append(s_re); si_p.append(s_im)
            q, k, v, u = _in_proj(xs, gain, w_in, q_gain, k_gain, gmat, width=MOBA_WIDTH, has_u=True)
            shs = lambda a: a.reshape(dbatch, dseq, MOBA_WIDTH)
            o_a = _sample_attn("moba", page_table, shs(q), shs(k), shs(v), bias_rows,
                               moba_k_pages, moba_v_pages, li)
            tabs = _s5_tables(*s5_args, chunk=dseq)
            y, s_re, s_im = _s5(u, tabs, state_s5_re[li], state_s5_im[li], batch=dbatch, seq=dseq)
            xs = _ab_out(xs, o_a.reshape(dbatch * dseq, MOBA_WIDTH), y, w_glu, w_out)
            mk_s.append(k.reshape(dbatch, dseq, 8, HEAD_DIM)); mv_s.append(v.reshape(dbatch, dseq, 8, HEAD_DIM))
            sr_s.append(s_re); si_s.append(s_im)
        else:
            lam_init = 0.8 - 0.6 * math.exp(-0.3 * layer)
            gain = row(c_norm[li])
            w_in = w_c_in[li].astype(BF16)
            w_out = w_c_out[li].astype(BF16)
            q_gain = row(jnp.tile(diff_q_gain[li], D_MODEL // HEAD_DIM))
            k_gain = row(jnp.tile(diff_k_gain[li], D_MODEL // HEAD_DIM))
            lams = (row(diff_lq1[li]), row(diff_lk1[li]), row(diff_lq2[li]), row(diff_lk2[li]))
            head_gain = row(diff_head_gain[li])
            qa, qb, kb, kt, v, vt = _in_proj(xp, gain, w_in, q_gain, k_gain, gmat,
                                             width=D_MODEL, has_u=False, seq=seq)
            shp = lambda a: a.reshape(batch, seq, D_MODEL)
            o = _attn_prompt("diff", shp(qa), shp(qb), shp(kb), vt, bias_tiles,
                             lams + (head_gain,), batch=batch, lam_init=lam_init)
            xp = _c_out(xp, o.reshape(batch * seq, D_MODEL), w_out)
            dk_p.append(kt.reshape(batch, 8, 2, HEAD_DIM, seq).transpose(0, 4, 1, 2, 3))
            dv_p.append(v.reshape(batch, seq, 8, 2 * HEAD_DIM))
            q, k, v = _in_proj(xs, gain, w_in, q_gain, k_gain, gmat, width=D_MODEL, has_u=False)
            shs = lambda a: a.reshape(dbatch, dseq, D_MODEL)
            o = _sample_attn("diff", page_table, shs(q), shs(k), shs(v), diff_bias_rows,
                             diff_k_pages, diff_v_pages, li, extra=lams + (head_gain,),
                             lam_init=lam_init)
            xs = _c_out(xs, o.reshape(dbatch * dseq, D_MODEL), w_out)
            dk_s.append(k.reshape(dbatch, dseq, 8, 2, HEAD_DIM)); dv_s.append(v.reshape(dbatch, dseq, 8, 2 * HEAD_DIM))
        weights = _ffn_weights(ffn_norm[layer], w_ffn_up[layer], ffn_conv_w[layer], ffn_conv_b[layer],
                               w_ffn_down[layer])
        yp, buf_p = _ffn_prompt(xp.reshape(batch, seq, D_MODEL), weights, batch=batch, seq=seq)
        xp = yp.reshape(batch * seq, D_MODEL)
        xs, buf_s = _ffn_sample(xs, state_ffn_conv[layer], weights, batch=dbatch, seq=dseq)
        cb_p.append(buf_p); cb_s.append(buf_s)

    st = lambda xs_, ax: jnp.stack(xs_, axis=ax)
    return (xp.reshape(batch, seq, D_MODEL), xs.reshape(dbatch, dseq, D_MODEL),
            st(mk_p, 1), st(mk_s, 1), st(mv_p, 1), st(mv_s, 1),
            st(sr_p, 0), st(sr_s, 0), st(si_p, 0), st(si_s, 0),
            st(dk_p, 1), st(dk_s, 1), st(dv_p, 1), st(dv_s, 1),
            st(cb_p, 0), st(cb_s, 0))
```

```python
import functools
import math

import jax
import jax.numpy as jnp
from jax import lax
from jax.experimental import pallas as pl
from jax.experimental.pallas import tpu as pltpu

F32 = jnp.float32
BF16 = jnp.bfloat16

D_MODEL = 1024
HEAD_DIM = 64
MOBA_WIDTH = 512
MOBA_BLOCK = 256
MOBA_TOPK = 3
S5_WIDTH = 512
S5_GROUP = 16
S5_GROUPS = 32
S5_STATE = 64
DIFF_HEADS = 8
REL_BUCKETS = 32
REL_MAX_DIST = 128
D_FF = 2816
RMS_EPS = 1e-6
PAGE_SIZE = 128

ATT_TILE = 256
FF_CHUNK = 256
N_FF_CHUNKS = D_FF // FF_CHUNK
M_INIT = -1e30
LOG2E = math.log2(math.e)
VT_ROWS = 128 + 16
VMEM_LIMIT = 56 * 1024 * 1024

_NT = (((1,), (1,)), ((), ()))


def _log2(n):
    assert n & (n - 1) == 0
    return n.bit_length() - 1


def _cparams(sem, vmem=VMEM_LIMIT):
    return pltpu.CompilerParams(dimension_semantics=sem, vmem_limit_bytes=vmem)


def _split(a):
    hi = a.astype(BF16)
    lo = (a - hi.astype(F32)).astype(BF16)
    return hi, lo


def _dot3(a, b):
    ah, al = _split(a)
    bh, bl = _split(b)
    d = functools.partial(jnp.dot, preferred_element_type=F32)
    return d(ah, bh) + d(ah, bl) + d(al, bh)


def _dot3_nt(a, b):
    ah, al = _split(a)
    bh, bl = _split(b)
    d = functools.partial(lax.dot_general, dimension_numbers=_NT, preferred_element_type=F32)
    return d(ah, bh) + d(ah, bl) + d(al, bh)


def _rms_rows(x, g):
    ms = jnp.mean(x * x, axis=-1, keepdims=True)
    return x * lax.rsqrt(ms + RMS_EPS) * g


def _group_mean_sq(t, gmat):
    t2 = t * t
    hi, lo = _split(t2)
    return (jnp.dot(hi, gmat, preferred_element_type=F32)
            + jnp.dot(lo, gmat, preferred_element_type=F32))


def _head_norm(t, gmat, gain):
    slab = gmat.shape[0]
    parts = [_group_mean_sq(t[:, s:s + slab], gmat) for s in range(0, t.shape[-1], slab)]
    ms = jnp.concatenate(parts, axis=1)
    return t * lax.rsqrt(ms + RMS_EPS) * gain


def _sigmoid(x):
    return 1.0 / (1.0 + jnp.exp(-x))


def _rel_bucket(d):
    dist = jnp.maximum(d, 0)
    exact = REL_BUCKETS // 2
    log_ratio = jnp.log(jnp.maximum(dist, 1).astype(F32) / exact) / math.log(REL_MAX_DIST / exact)
    large = exact + (log_ratio * (REL_BUCKETS - exact)).astype(jnp.int32)
    return jnp.where(dist < exact, dist, jnp.minimum(large, REL_BUCKETS - 1))


def _bias_lookup(bucket, tab_ref, h):
    out = jnp.zeros(bucket.shape, F32)
    for b in range(REL_BUCKETS):
        out = jnp.where(bucket == b, tab_ref[b, h], out)
    return out


def _bias_tiles_kernel(tab_ref, o_ref):
    h = pl.program_id(0)
    key = lax.broadcasted_iota(jnp.int32, (ATT_TILE, ATT_TILE), 0)
    qry = lax.broadcasted_iota(jnp.int32, (ATT_TILE, ATT_TILE), 1)
    d0 = qry - key
    far = tab_ref[REL_BUCKETS - 1, h]
    near0 = (_bias_lookup(_rel_bucket(d0), tab_ref, h) - far) * LOG2E
    o_ref[0, 0] = jnp.where(d0 >= 0, near0, -jnp.inf)
    o_ref[0, 1] = (_bias_lookup(_rel_bucket(d0 + ATT_TILE), tab_ref, h) - far) * LOG2E


def _bias_sample_kernel(tab_ref, o_ref, *, past_len, n_new):
    h = pl.program_id(0)
    shape = o_ref.shape[1:]
    i = lax.broadcasted_iota(jnp.int32, shape, 0)
    k = lax.broadcasted_iota(jnp.int32, shape, 1)
    d = past_len + i - k
    ok = (d >= 0) & (k < past_len + n_new)
    o_ref[0] = jnp.where(ok, _bias_lookup(_rel_bucket(d), tab_ref, h), -jnp.inf)


def _build_bias(rel_bias, past_len, n_new):
    n_heads = rel_bias.shape[1]
    smem = pl.BlockSpec(memory_space=pltpu.SMEM)
    tiles = pl.pallas_call(
        _bias_tiles_kernel,
        grid=(n_heads,),
        in_specs=[smem],
        out_specs=pl.BlockSpec((1, 2, ATT_TILE, ATT_TILE), lambda h: (h, 0, 0, 0)),
        out_shape=jax.ShapeDtypeStruct((n_heads, 2, ATT_TILE, ATT_TILE), F32),
        compiler_params=_cparams(("arbitrary",)),
        name="bias_tiles",
    )(rel_bias)
    width = past_len + PAGE_SIZE
    sample = pl.pallas_call(
        functools.partial(_bias_sample_kernel, past_len=past_len, n_new=n_new),
        grid=(n_heads,),
        in_specs=[smem],
        out_specs=pl.BlockSpec((1, n_new, width), lambda h: (h, 0, 0)),
        out_shape=jax.ShapeDtypeStruct((n_heads, n_new, width), F32),
        compiler_params=_cparams(("arbitrary",)),
        name="bias_sample",
    )(rel_bias)
    return tiles, sample.reshape(n_heads * n_new, width)


def _first_half_mask(shape):
    lane = lax.broadcasted_iota(jnp.int32, shape, 1)
    return (lane & HEAD_DIM) == 0


def _in_proj_kernel(x_ref, g_ref, w_ref, qg_ref, kg_ref, gm_ref, *outs, tm, width, has_u, prompt):
    h = _rms_rows(x_ref[...], g_ref[...]).astype(BF16)
    y = jnp.dot(h, w_ref[...], preferred_element_type=F32)
    gm = gm_ref[...]
    qn = _head_norm(y[:, 0:width], gm, qg_ref[...]) * (HEAD_DIM ** -0.5)
    kn = _head_norm(y[:, width:2 * width], gm, kg_ref[...])
    v = y[:, 2 * width:3 * width]
    if not prompt:
        outs[0][...] = qn
        outs[1][...] = kn
        outs[2][...] = v
        if has_u:
            outs[3][...] = y[:, 3 * width:4 * width]
        return
    qa_ref, qb_ref, kb_ref, kt_ref, v_ref, vt_ref = outs[2:8]
    outs = outs[2:]
    first = _first_half_mask(qn.shape)
    ql = qn * LOG2E
    qa_ref[...] = jnp.where(first, ql, 0.0).astype(BF16)
    qb_ref[...] = jnp.where(first, 0.0, ql).astype(BF16)
    kb_ref[...] = kn.astype(BF16)
    kt_ref[0, 0] = kn.T
    v_t = v.T
    v_ref[0, 0] = v_t if has_u else v
    v_tb = v_t.astype(BF16)
    ones = jnp.ones((VT_ROWS - 128, ATT_TILE), BF16)
    for g in range(width // 128):
        for j in range(tm // ATT_TILE):
            vt_ref[0, g, j] = jnp.concatenate(
                [v_tb[g * 128:(g + 1) * 128, j * ATT_TILE:(j + 1) * ATT_TILE], ones], axis=0)
    if has_u:
        u_ref, km_ref = outs[6:]
        u_ref[...] = y[:, 3 * width:4 * width]
        for j in range(tm // MOBA_BLOCK):
            km_ref[j] = jnp.mean(kn[j * MOBA_BLOCK:(j + 1) * MOBA_BLOCK], axis=0, keepdims=True)


NORM_SLAB = 256


def _group_matrix():
    i = jnp.arange(NORM_SLAB) // HEAD_DIM
    return jnp.where(i[:, None] == i[None, :], 1.0 / HEAD_DIM, 0.0).astype(BF16)


def _row_spec(tm, width):
    return pl.BlockSpec((tm, width), lambda i: (i, 0))


def _const_spec(shape):
    nd = len(shape)
    return pl.BlockSpec(shape, lambda *_: (0,) * nd)


def _in_proj(x, gain, w_bf, q_gain, k_gain, gmat, *, width, has_u, seq=None, rows_all=None, li=0):
    n = x.shape[0]
    prompt = seq is not None
    tm = min(512, n)
    sds = jax.ShapeDtypeStruct
    n_out = 4 if has_u else 3
    in_specs = [_row_spec(tm, D_MODEL), _const_spec((1, D_MODEL)), _const_spec((D_MODEL, n_out * width)),
                _const_spec((1, width)), _const_spec((1, width)), _const_spec((NORM_SLAB, NORM_SLAB))]
    args = (x, gain, w_bf, q_gain, k_gain, gmat)
    aliases = {}
    if not prompt:
        out_shape = tuple(sds((n, width), F32) for _ in range(n_out))
        out_specs = tuple(_row_spec(tm, width) for _ in range(n_out))
    else:
        tps = seq // tm
        batch = n // seq
        groups = width // 128
        feat_spec = pl.BlockSpec((1, 1, width, tm), lambda i: (i // tps, li, 0, i % tps))
        tok_spec = pl.BlockSpec((1, 1, tm, width), lambda i: (i // tps, li, i % tps, 0))
        out_shape = [sds((n, width), BF16), sds((n, width), BF16), sds((n, width), BF16),
                     sds(rows_all[0].shape, F32), sds(rows_all[1].shape, F32),
                     sds((batch, groups, seq // ATT_TILE, VT_ROWS, ATT_TILE), BF16)]
        out_specs = [_row_spec(tm, width)] * 3 + [
            feat_spec, feat_spec if has_u else tok_spec,
            pl.BlockSpec((1, groups, tm // ATT_TILE, VT_ROWS, ATT_TILE),
                         lambda i: (i // tps, 0, i % tps, 0, 0))]
        if has_u:
            out_shape += [sds((n, width), F32), sds((n // MOBA_BLOCK, 1, width), F32)]
            out_specs += [_row_spec(tm, width),
                          pl.BlockSpec((tm // MOBA_BLOCK, 1, width), lambda i: (i, 0, 0))]
        in_specs += [pl.BlockSpec(memory_space=pl.ANY)] * 2
        args += tuple(rows_all)
        aliases = {6: 3, 7: 4}
    return pl.pallas_call(
        functools.partial(_in_proj_kernel, tm=tm, width=width, has_u=has_u, prompt=prompt),
        grid=(n // tm,),
        in_specs=in_specs,
        out_specs=tuple(out_specs),
        out_shape=tuple(out_shape),
        input_output_aliases=aliases,
        compiler_params=_cparams(("arbitrary",)),
        name="ab_in" if has_u else "c_in",
    )(*args)


def _attn_prompt_kernel(*refs, mode, lam_init):
    if mode == "moba":
        qa_ref, qb_ref, k_ref, vt_ref, bias_ref, km_ref, o_ref, acc_ref, sel_ref = refs
    else:
        (qa_ref, qb_ref, k_ref, vt_ref, bias_ref, lq1_ref, lk1_ref, lq2_ref, lk2_ref,
         hg_ref, o_ref, acc_ref) = refs
    qi = pl.program_id(2)
    t = ATT_TILE
    q2 = jnp.concatenate([qa_ref[0], qb_ref[0]], axis=0)
    if mode == "moba":
        tile = lambda j: jnp.concatenate([bias_ref[0, j], bias_ref[1, j]], axis=1)
    else:
        tile = lambda j: jnp.concatenate([bias_ref[0, j], bias_ref[0, j]], axis=1)

    acc_ref[...] = jnp.zeros(acc_ref.shape, F32)

    if mode == "moba":
        km_hi, km_lo = _split(km_ref[0])
        gate = (lax.dot_general(km_hi, q2, _NT, preferred_element_type=F32)
                + lax.dot_general(km_lo, q2, _NT, preferred_element_type=F32))
        blk_i = lax.broadcasted_iota(jnp.int32, gate.shape, 0)
        blk_f = blk_i.astype(F32)
        valid = blk_i < qi
        cur = jnp.where(valid, gate, -jnp.inf)
        sel_pen = jnp.full(gate.shape, -jnp.inf, F32)
        for _ in range(MOBA_TOPK):
            top = jnp.max(cur, axis=0, keepdims=True)
            idx = jnp.min(jnp.where(cur == top, blk_f, 1e9), axis=0, keepdims=True)
            pick = blk_f == idx
            sel_pen = jnp.where(pick, jnp.where(valid, 0.0, -jnp.inf), sel_pen)
            cur = jnp.where(pick, -jnp.inf, cur)
        sel_ref[...] = sel_pen
        block_pen = lambda kj: sel_ref[pl.ds(kj, 1), :]
    else:
        block_pen = None

    def tiles(m_prev, kjs, bias_adds):
        scores = []
        for kj, bias_add in zip(kjs, bias_adds):
            start = pl.multiple_of(kj * t, t)
            s = lax.dot_general(k_ref[0, pl.ds(start, t), :], q2, _NT,
                                preferred_element_type=F32)
            scores.append(s if bias_add is None else s + bias_add)
        parts = []
        for s in scores:
            m_t = jnp.maximum(jnp.max(s, axis=0, keepdims=True), M_INIT)
            parts.append((m_t, jnp.exp2(s - m_t).astype(BF16)))
        m_new = m_prev
        for m_t, _ in parts:
            m_new = jnp.maximum(m_new, m_t)
        acc = jnp.exp2(m_prev - m_new) * acc_ref[...]
        for kj, (m_t, p) in zip(kjs, parts):
            acc = acc + jnp.exp2(m_t - m_new) * jnp.dot(vt_ref[0, 0, kj], p,
                                                        preferred_element_type=F32)
        acc_ref[...] = acc
        return m_new

    def far_pen(kj):
        return None if block_pen is None else block_pen(kj)

    def far_run(first, count, m_prev):
        kjs = [first + j for j in range(count)]
        return tiles(m_prev, kjs, [far_pen(kj) for kj in kjs])

    n_far = jnp.maximum(qi - 1, 0)
    m = jnp.full((1, 2 * t), M_INIT, F32)
    n_quads = lax.shift_right_logical(n_far, 2)
    m = lax.fori_loop(0, n_quads, lambda i, c: far_run(4 * i, 4, c), m)

    def last_group(n_left):
        def run(m_prev):
            kn = jnp.maximum(qi - 1, 0)
            near = tile(1) + jnp.where(qi >= 1, 0.0, -jnp.inf)
            if block_pen is not None:
                near = near + block_pen(kn)
            left = [4 * n_quads + j for j in range(n_left)]
            return tiles(m_prev, left + [kn, qi], [far_pen(kj) for kj in left] + [near, tile(0)])
        return run

    lax.switch(n_far & 3, [last_group(n) for n in range(4)], m)

    out = acc_ref[0:128, :] / acc_ref[128:129, :]
    if mode == "moba":
        row = lax.broadcasted_iota(jnp.int32, (128, t), 0)
        o_ref[0] = jnp.where(row < HEAD_DIM, out[:, :t], out[:, t:]).T
    else:
        l1 = jnp.sum(lq1_ref[...] * lk1_ref[...], axis=1, keepdims=True)
        l2 = jnp.sum(lq2_ref[...] * lk2_ref[...], axis=1, keepdims=True)
        lam = jnp.exp(l1) - jnp.exp(l2) + lam_init
        o = (out[:, :t] - lam * out[:, t:]).T
        o_ref[0] = _rms_rows(o, hg_ref[...]) * (1.0 - lam_init)


def _attn_prompt(mode, qa, qb, kb, vt, bias_tiles, extra, *, batch, lam_init=0.0):
    _, seq, width = qa.shape
    groups = width // 128
    nq = seq // ATT_TILE
    q_spec = pl.BlockSpec((1, ATT_TILE, 128), lambda b, g, i: (b, i, g))
    k_spec = pl.BlockSpec((1, seq, 128), lambda b, g, i: (b, 0, g))
    vt_spec = pl.BlockSpec((1, 1, nq, VT_ROWS, ATT_TILE), lambda b, g, i: (b, g, 0, 0, 0))
    scratch = [pltpu.VMEM((VT_ROWS, 2 * ATT_TILE), F32)]
    if mode == "moba":
        (kmean,) = extra
        nb = kmean.shape[1]
        assert nb % 8 == 0
        bias_spec = pl.BlockSpec((2, 2, ATT_TILE, ATT_TILE), lambda b, g, i: (g, 0, 0, 0))
        extra_specs = [pl.BlockSpec((1, nb, 128), lambda b, g, i: (b, 0, g))]
        scratch.append(pltpu.VMEM((nb, 2 * ATT_TILE), F32))
    else:
        bias_spec = pl.BlockSpec((1, 2, ATT_TILE, ATT_TILE), lambda b, g, i: (g, 0, 0, 0))
        extra_specs = [_const_spec((1, HEAD_DIM))] * 4 + [_const_spec((1, 128))]
    return pl.pallas_call(
        functools.partial(_attn_prompt_kernel, mode=mode, lam_init=lam_init),
        grid=(batch, groups, nq),
        in_specs=[q_spec, q_spec, k_spec, vt_spec, bias_spec] + extra_specs,
        out_specs=pl.BlockSpec((1, ATT_TILE, 128), lambda b, g, i: (b, i, g)),
        out_shape=jax.ShapeDtypeStruct((batch, seq, width), F32),
        scratch_shapes=scratch,
        compiler_params=_cparams(("arbitrary", "arbitrary", "arbitrary")),
        name=f"{mode}_prompt",
    )(qa, qb, kb, vt, bias_tiles, *extra)


def _page_map(b, pt_ref, *, j, li):
    return (pt_ref[b, j], li, 0, 0)


def _pad_rows(x, rows):
    return jnp.concatenate([x, jnp.zeros((rows - x.shape[0], x.shape[1]), x.dtype)], axis=0)


def _softmax_pv(scores, values, transposed):
    mx = scores[0]
    for s in scores[1:]:
        mx = jnp.maximum(mx, s)
    m = jnp.max(mx, axis=1, keepdims=True)
    tot = None
    acc = None
    for s, v, tr in zip(scores, values, transposed):
        p = jnp.exp(s - m)
        tot = p if tot is None else tot + p
        if tr:
            d = lax.dot_general(p.astype(BF16), v, _NT, preferred_element_type=F32)
        else:
            d = jnp.dot(p.astype(BF16), v, preferred_element_type=F32)
        acc = d if acc is None else acc + d
    return acc / jnp.sum(tot, axis=1, keepdims=True)


def _moba_sample_kernel(pt_ref, q_ref, kn_ref, vn_ref, bias_ref, *rest, n_pages):
    kp = rest[:n_pages]
    vp = rest[n_pages:2 * n_pages]
    o_ref = rest[2 * n_pages]
    nq = q_ref.shape[1]
    heads = MOBA_WIDTH // HEAD_DIM
    rows = heads * nq
    q = q_ref[0]
    q_rep = jnp.concatenate([q] * heads, axis=0)
    row = lax.broadcasted_iota(jnp.int32, (rows, MOBA_WIDTH), 0)
    lane = lax.broadcasted_iota(jnp.int32, (rows, MOBA_WIDTH), 1)
    own_head = (row >> _log2(nq)) == (lane >> _log2(HEAD_DIM))
    qbd = jnp.where(own_head, q_rep, 0.0).astype(BF16)

    pages_per_block = MOBA_BLOCK // PAGE_SIZE
    n_blocks = n_pages // pages_per_block
    scores = [jnp.dot(qbd, kp[j][0, 0].astype(BF16), preferred_element_type=F32)
              for j in range(n_pages)]
    lane_i = lax.broadcasted_iota(jnp.int32, (rows, 128), 1)
    gate = jnp.zeros((rows, 128), F32)
    for n in range(n_blocks):
        blk = scores[n * pages_per_block]
        for j in range(n * pages_per_block + 1, (n + 1) * pages_per_block):
            blk = blk + scores[j]
        gate = jnp.where(lane_i == n, jnp.sum(blk, axis=1, keepdims=True) * (1.0 / MOBA_BLOCK), gate)
    lane_f = lane_i.astype(F32)
    valid = lane_i < n_blocks
    cur = jnp.where(valid, gate, -jnp.inf)
    sel_pen = jnp.full(gate.shape, -jnp.inf, F32)
    for _ in range(MOBA_TOPK):
        top = jnp.max(cur, axis=1, keepdims=True)
        idx = jnp.min(jnp.where(cur == top, lane_f, 1e9), axis=1, keepdims=True)
        pick = lane_f == idx
        sel_pen = jnp.where(pick, jnp.where(valid, 0.0, -jnp.inf), sel_pen)
        cur = jnp.where(pick, -jnp.inf, cur)

    for j in range(n_pages):
        n = j // pages_per_block
        scores[j] = scores[j] + bias_ref[:, j * PAGE_SIZE:(j + 1) * PAGE_SIZE] + sel_pen[:, n:n + 1]
    values = [vp[j][0, 0].astype(BF16) for j in range(n_pages)]
    k_new = _pad_rows(kn_ref[0], PAGE_SIZE).astype(BF16)
    scores.append(lax.dot_general(qbd, k_new, _NT, preferred_element_type=F32)
                  + bias_ref[:, n_pages * PAGE_SIZE:(n_pages + 1) * PAGE_SIZE])
    values.append(_pad_rows(vn_ref[0], PAGE_SIZE).astype(BF16))

    res = _softmax_pv(scores, values, [True] * n_pages + [False])
    res = jnp.where(own_head, res, 0.0)
    out = res[0:nq]
    for h in range(1, heads):
        out = out + res[h * nq:(h + 1) * nq]
    o_ref[0] = out


def _diff_sample_kernel(pt_ref, q_ref, kn_ref, vn_ref, bias_ref, lq1_ref, lk1_ref, lq2_ref, lk2_ref,
                        hg_ref, *rest, n_pages, lam_init):
    kp = rest[:n_pages]
    vp = rest[n_pages:2 * n_pages]
    o_ref = rest[2 * n_pages]
    nq = q_ref.shape[1]
    heads = DIFF_HEADS
    hw = 2 * HEAD_DIM
    l1 = jnp.sum(lq1_ref[...] * lk1_ref[...], axis=1, keepdims=True)
    l2 = jnp.sum(lq2_ref[...] * lk2_ref[...], axis=1, keepdims=True)
    lam = jnp.exp(l1) - jnp.exp(l2) + lam_init
    lane = lax.broadcasted_iota(jnp.int32, (nq, hw), 1)
    new_cols = slice(n_pages * PAGE_SIZE, (n_pages + 1) * PAGE_SIZE)
    for h in range(heads):
        cols = slice(h * hw, (h + 1) * hw)
        qh = q_ref[0, :, cols]
        q2 = jnp.concatenate([jnp.where(lane < HEAD_DIM, qh, 0.0),
                              jnp.where(lane < HEAD_DIM, 0.0, qh)], axis=0).astype(BF16)
        bias = bias_ref[2 * nq * h:2 * nq * (h + 1), :]
        scores = []
        values = []
        for j in range(n_pages):
            kj = kp[j][0, 0, cols, :].astype(BF16)
            scores.append(jnp.dot(q2, kj, preferred_element_type=F32)
                          + bias[:, j * PAGE_SIZE:(j + 1) * PAGE_SIZE])
            values.append(vp[j][0, 0, pl.ds(h, PAGE_SIZE, stride=heads), :].astype(BF16))
        k_new = _pad_rows(kn_ref[0, :, cols], PAGE_SIZE).astype(BF16)
        scores.append(lax.dot_general(q2, k_new, _NT, preferred_element_type=F32) + bias[:, new_cols])
        values.append(_pad_rows(vn_ref[0, :, cols], PAGE_SIZE).astype(BF16))
        res = _softmax_pv(scores, values, [False] * (n_pages + 1))
        d = res[:nq] - lam * res[nq:]
        o_ref[0, :, cols] = _rms_rows(d, hg_ref[...]) * (1.0 - lam_init)


def _sample_attn(mode, page_table, q, k_new, v_new, bias_rows, cache_k, cache_v, li, extra=(),
                 lam_init=0.0):
    batch, n_new, width = q.shape
    n_pages = page_table.shape[1]
    tok_spec = pl.BlockSpec((1, n_new, width), lambda b, pt: (b, 0, 0))
    const = lambda shape: pl.BlockSpec(shape, lambda b, pt: (0,) * len(shape))
    page_specs = [pl.BlockSpec((1, 1, width, PAGE_SIZE), functools.partial(_page_map, j=j, li=li))
                  for j in range(n_pages)]
    if mode == "moba":
        body = functools.partial(_moba_sample_kernel, n_pages=n_pages)
        extra_specs = []
    else:
        body = functools.partial(_diff_sample_kernel, n_pages=n_pages, lam_init=lam_init)
        extra_specs = [const((1, HEAD_DIM))] * 4 + [const((1, 2 * HEAD_DIM))]
    grid_spec = pltpu.PrefetchScalarGridSpec(
        num_scalar_prefetch=1,
        grid=(batch,),
        in_specs=[tok_spec, tok_spec, tok_spec, const(bias_rows.shape)] + extra_specs
                 + page_specs + page_specs,
        out_specs=tok_spec,
    )
    return pl.pallas_call(
        body,
        grid_spec=grid_spec,
        out_shape=jax.ShapeDtypeStruct((batch, n_new, width), F32),
        compiler_params=_cparams(("arbitrary",)),
        name=f"{mode}_sample",
    )(page_table, q, k_new, v_new, bias_rows, *extra, *([cache_k] * n_pages), *([cache_v] * n_pages))


def _s5_tables_kernel(are_ref, aim_ref, ldt_ref, btre_ref, btim_ref, cre_ref, cim_ref,
                      xre_ref, xim_ref, zre_ref, zim_ref, wre_ref, wim_ref, vre_ref, vim_ref,
                      lre_ref, lim_ref, *, chunk):
    a_re = are_ref[...]
    a_im = aim_ref[...]
    dt = jnp.exp(ldt_ref[...])
    mag = jnp.exp(a_re * dt)
    lb_re = mag * jnp.cos(a_im * dt)
    lb_im = mag * jnp.sin(a_im * dt)
    den = a_re * a_re + a_im * a_im
    f_re = ((lb_re - 1.0) * a_re + lb_im * a_im) / den
    f_im = (lb_im * a_re - (lb_re - 1.0) * a_im) / den
    bb_re = f_re * btre_ref[...] - f_im * btim_ref[...]
    bb_im = f_re * btim_ref[...] + f_im * btre_ref[...]
    c_re = cre_ref[...]
    c_im = cim_ref[...]
    inv_mag = jnp.exp(-(a_re * dt))
    il_re = inv_mag * jnp.cos(a_im * dt)
    il_im = -inv_mag * jnp.sin(a_im * dt)

    def cmul(x_re, x_im, y_re, y_im):
        return x_re * y_re - x_im * y_im, x_re * y_im + x_im * y_re

    def put(ref, j, val):
        ref[:, j * S5_GROUP:(j + 1) * S5_GROUP, :] = val.reshape(S5_GROUPS, S5_GROUP, S5_STATE)

    one = jnp.ones_like(a_re)
    zero = jnp.zeros_like(a_re)
    pos = [(one, zero)]
    neg = [(one, zero)]
    for _ in range(chunk):
        pos.append(cmul(pos[-1][0], pos[-1][1], lb_re, lb_im))
        neg.append(cmul(neg[-1][0], neg[-1][1], il_re, il_im))
    for j in range(chunk):
        x = cmul(bb_re, bb_im, *neg[j])
        z = cmul(c_re, c_im, *pos[j])
        w = cmul(bb_re, bb_im, *pos[chunk - 1 - j])
        v = cmul(c_re, c_im, *pos[j + 1])
        put(xre_ref, j, x[0]); put(xim_ref, j, x[1])
        put(zre_ref, j, z[0]); put(zim_ref, j, z[1])
        put(wre_ref, j, w[0]); put(wim_ref, j, w[1])
        put(vre_ref, j, v[0]); put(vim_ref, j, -v[1])
    lre_ref[...] = pos[chunk][0].reshape(S5_GROUPS, S5_GROUP, S5_STATE)[:, 0:1, :]
    lim_ref[...] = pos[chunk][1].reshape(S5_GROUPS, S5_GROUP, S5_STATE)[:, 0:1, :]


def _s5_toeplitz_kernel(xre_ref, xim_ref, zre_ref, zim_ref, m_ref):
    m = _dot3_nt(xre_ref[0], zre_ref[0]) - _dot3_nt(xim_ref[0], zim_ref[0])
    r = lax.broadcasted_iota(jnp.int32, m.shape, 0) >> _log2(S5_GROUP)
    c = lax.broadcasted_iota(jnp.int32, m.shape, 1) >> _log2(S5_GROUP)
    m_ref[0] = jnp.where(c >= r, m, 0.0)


def _s5_tables(a_re, a_im, log_dt, b_re, b_im, c_re, c_im, d_skip, chunk):
    g, n, p = S5_GROUPS, S5_STATE, S5_GROUP
    kl = chunk * p
    rep = lambda x: jnp.repeat(x, p, axis=0)
    flat = lambda x: x.reshape(g * p, n)
    ins = (rep(a_re), rep(a_im), rep(jnp.broadcast_to(log_dt[:, None], (g, n))),
           flat(b_re.transpose(0, 2, 1)), flat(b_im.transpose(0, 2, 1)), flat(c_re), flat(c_im))
    big = jax.ShapeDtypeStruct((g, kl, n), F32)
    small = jax.ShapeDtypeStruct((g, 1, n), F32)
    outs = pl.pallas_call(
        functools.partial(_s5_tables_kernel, chunk=chunk),
        out_shape=(big,) * 8 + (small, small),
        compiler_params=pltpu.CompilerParams(vmem_limit_bytes=VMEM_LIMIT),
        name="s5_tables",
    )(*ins)
    x_re, x_im, z_re, z_im, w_re, w_im, v_re, v_im, l_re, l_im = outs
    gspec = pl.BlockSpec((1, kl, n), lambda i: (i, 0, 0))
    toep = pl.pallas_call(
        _s5_toeplitz_kernel,
        grid=(g,),
        in_specs=[gspec] * 4,
        out_specs=pl.BlockSpec((1, kl, kl), lambda i: (i, 0, 0)),
        out_shape=jax.ShapeDtypeStruct((g, kl, kl), F32),
        compiler_params=_cparams(("arbitrary",)),
        name="s5_toeplitz",
    )(x_re, x_im, z_re, z_im)

    def pad_lo(x):
        return jnp.pad(x, [(0, 0)] * (x.ndim - 1) + [(0, n)])

    def pad_hi(x):
        return jnp.pad(x, [(0, 0)] * (x.ndim - 1) + [(n, 0)])

    def pair_rows(w):
        w = w.reshape(g // 2, 2, kl, n)
        return jnp.concatenate([pad_lo(w[:, 0]), pad_hi(w[:, 1])], axis=1)

    def pair_pad(v):
        v = v.reshape(g // 2, 2, kl, n)
        return jnp.stack([pad_lo(v[:, 0]), pad_hi(v[:, 1])], axis=1).reshape(g, kl, 2 * n)

    def pair_lanes(l):
        l = l.reshape(g // 2, 2, 1, n)
        return jnp.concatenate([l[:, 0], l[:, 1]], axis=-1)

    d_tile = jnp.tile(d_skip, (1, chunk)).reshape(g, 1, kl)
    return dict(chunk=chunk, toep=toep, w_re=pair_rows(w_re), w_im=pair_rows(w_im),
                v_re=pair_pad(v_re), v_im=pair_pad(v_im), d=d_tile,
                l_re=pair_lanes(l_re), l_im=pair_lanes(l_im))


S5_GROUPS_PER_STEP = 4


def _s5_apply_kernel(u_ref, m_ref, wre_ref, wim_ref, vre_ref, vim_ref, d_ref, lre_ref, lim_ref,
                     s0re_ref, s0im_ref, y_ref, stre_ref, stim_ref, ere_ref, eim_ref, sre_ref, sim_ref,
                     *, n_chunks, batch):
    pairs = S5_GROUPS_PER_STEP // 2
    for k in range(pairs):
        ucat = jnp.concatenate([u_ref[2 * k], u_ref[2 * k + 1]], axis=1)
        ere_ref[k] = _dot3(ucat, wre_ref[k])
        eim_ref[k] = _dot3(ucat, wim_ref[k])
    l_re = lre_ref[...]
    l_im = lim_ref[...]

    def step(c, carry):
        s_re, s_im = carry
        rows = pl.ds(c * batch, batch)
        sre_ref[:, rows, :] = s_re
        sim_ref[:, rows, :] = s_im
        n_re = l_re * s_re - l_im * s_im + ere_ref[:, rows, :]
        n_im = l_re * s_im + l_im * s_re + eim_ref[:, rows, :]
        return n_re, n_im

    s_re, s_im = lax.fori_loop(0, n_chunks, step, (s0re_ref[...], s0im_ref[...]))
    stre_ref[...] = s_re
    stim_ref[...] = s_im
    for g in range(S5_GROUPS_PER_STEP):
        u = u_ref[g]
        y_ref[g] = (_dot3(u, m_ref[g]) + _dot3_nt(sre_ref[g // 2], vre_ref[g])
                    + _dot3_nt(sim_ref[g // 2], vim_ref[g]) + d_ref[g] * u)


def _s5_apply(u_chunks, tabs, s0_re, s0_im, *, n_chunks, batch):
    g, rows, kl = u_chunks.shape
    gs = S5_GROUPS_PER_STEP
    ps = gs // 2
    gspec = lambda shape: pl.BlockSpec((gs,) + shape, lambda i: (i, 0, 0))
    pspec = lambda shape: pl.BlockSpec((ps,) + shape, lambda i: (i, 0, 0))
    return pl.pallas_call(
        functools.partial(_s5_apply_kernel, n_chunks=n_chunks, batch=batch),
        grid=(g // gs,),
        in_specs=[gspec((rows, kl)), gspec((kl, kl)), pspec((2 * kl, 128)), pspec((2 * kl, 128)),
                  gspec((kl, 128)), gspec((kl, 128)), gspec((1, kl)), pspec((1, 128)), pspec((1, 128)),
                  pspec((batch, 128)), pspec((batch, 128))],
        out_specs=(gspec((rows, kl)), pspec((batch, 128)), pspec((batch, 128))),
        out_shape=(jax.ShapeDtypeStruct((g, rows, kl), F32),
                   jax.ShapeDtypeStruct((g // 2, batch, 128), F32),
                   jax.ShapeDtypeStruct((g // 2, batch, 128), F32)),
        scratch_shapes=[pltpu.VMEM((ps, rows, 128), F32)] * 4,
        compiler_params=_cparams(("arbitrary",)),
        name="s5_apply",
    )(u_chunks, tabs["toep"], tabs["w_re"], tabs["w_im"], tabs["v_re"], tabs["v_im"], tabs["d"],
      tabs["l_re"], tabs["l_im"], s0_re, s0_im)


def _s5(u, tabs, s0_re, s0_im, *, batch, seq):
    chunk = tabs["chunk"]
    g, p, n = S5_GROUPS, S5_GROUP, S5_STATE
    n_chunks = seq // chunk
    uc = u.reshape(batch, n_chunks, chunk, g, p).transpose(3, 1, 0, 2, 4)
    uc = uc.reshape(g, n_chunks * batch, chunk * p)
    pack = lambda s: s.reshape(batch, g // 2, 2 * n).transpose(1, 0, 2)
    y, st_re, st_im = _s5_apply(uc, tabs, pack(s0_re), pack(s0_im), n_chunks=n_chunks, batch=batch)
    y = y.reshape(g, n_chunks, batch, chunk, p).transpose(2, 1, 3, 0, 4).reshape(batch * seq, g * p)
    unpack = lambda s: s.transpose(1, 0, 2).reshape(batch, g, n)
    return y, unpack(st_re), unpack(st_im)


def _gelu_tanh(x):
    return 0.5 * x * (1.0 + jnp.tanh(math.sqrt(2.0 / math.pi) * (x + 0.044715 * (x * x * x))))


def _ab_out_kernel(x_ref, oa_ref, y_ref, wglu_ref, w_ref, o_ref):
    gl = _gelu_tanh(y_ref[...])
    gate = jnp.dot(gl.astype(BF16), wglu_ref[...], preferred_element_type=F32)
    ob = gl * _sigmoid(gate)
    cat = jnp.concatenate([oa_ref[...].astype(BF16), ob.astype(BF16)], axis=1)
    o_ref[...] = x_ref[...] + jnp.dot(cat, w_ref[...], preferred_element_type=F32)


def _c_out_kernel(x_ref, o_in_ref, w_ref, o_ref):
    o_ref[...] = x_ref[...] + jnp.dot(o_in_ref[...].astype(BF16), w_ref[...],
                                      preferred_element_type=F32)


def _ab_out(x, o_a, y, wglu_bf, w_bf):
    n = x.shape[0]
    tm = min(512, n)
    return pl.pallas_call(
        _ab_out_kernel,
        grid=(n // tm,),
        in_specs=[_row_spec(tm, D_MODEL), _row_spec(tm, 512), _row_spec(tm, 512),
                  _const_spec((512, 512)), _const_spec((D_MODEL, D_MODEL))],
        out_specs=_row_spec(tm, D_MODEL),
        out_shape=jax.ShapeDtypeStruct((n, D_MODEL), F32),
        compiler_params=_cparams(("arbitrary",)),
        name="ab_out",
    )(x, o_a, y, wglu_bf, w_bf)


def _c_out(x, o, w_bf):
    n = x.shape[0]
    tm = min(512, n)
    return pl.pallas_call(
        _c_out_kernel,
        grid=(n // tm,),
        in_specs=[_row_spec(tm, D_MODEL), _row_spec(tm, D_MODEL), _const_spec((D_MODEL, D_MODEL))],
        out_specs=_row_spec(tm, D_MODEL),
        out_shape=jax.ShapeDtypeStruct((n, D_MODEL), F32),
        compiler_params=_cparams(("arbitrary",)),
        name="c_out",
    )(x, o, w_bf)


FF_SUB_BLOCKS = 4


def _ffn_kernel(*refs, tm, carry, period):
    if carry:
        x_ref, g_ref, wg_ref, wu_ref, cw_ref, wdn_ref, y_ref, cs_ref, hb_ref, acc_ref, halo_ref = refs
    else:
        (x_ref, g_ref, wg_ref, wu_ref, cw_ref, wdn_ref, old0_ref, old1_ref, y_ref, gout_ref,
         hb_ref, acc_ref) = refs
    i = pl.program_id(1)
    c = pl.program_id(2)

    @pl.when(c == 0)
    def _():
        x = x_ref[0]
        hb_ref[...] = _rms_rows(x, g_ref[...]).astype(BF16)
        acc_ref[...] = x

    sub = tm // FF_SUB_BLOCKS
    assert sub % 8 == 0 and (carry or sub % period == 0)
    blocks = [slice(s * sub, (s + 1) * sub) for s in range(FF_SUB_BLOCKS)]
    gates = [jnp.dot(hb_ref[r, :], wg_ref[...], preferred_element_type=F32) for r in blocks]
    ups = [jnp.dot(hb_ref[r, :], wu_ref[...], preferred_element_type=F32) for r in blocks]
    cw = cw_ref[...]
    row = lax.broadcasted_iota(jnp.int32, (sub, FF_CHUNK), 0)
    if carry:
        prev = jnp.where(i == 0, 0.0, halo_ref[c])
    for s, r in enumerate(blocks):
        gate = gates[s]
        r1 = pltpu.roll(gate, 1, 0)
        r2 = pltpu.roll(gate, 2, 0)
        if carry:
            p1 = jnp.where(row == 0, prev[7:8], r1)
            p2 = jnp.where(row == 0, prev[6:7], jnp.where(row == 1, prev[7:8], r2))
            prev = gate[sub - 8:sub]
        else:
            t = row & (period - 1)
            nseq = sub // period
            rep = lambda b: jnp.broadcast_to(b[:, None, :], (nseq, period, FF_CHUNK)).reshape(sub, FF_CHUNK)
            old0 = rep(old0_ref[s * nseq:(s + 1) * nseq, :])
            old1 = rep(old1_ref[s * nseq:(s + 1) * nseq, :])
            p1 = jnp.where(t == 0, old1, r1)
            p2 = jnp.where(t == 0, old0, jnp.where(t == 1, old1, r2))
            gout_ref[0, r, :] = gate
        conv = cw[3:4] + cw[0:1] * p2 + cw[1:2] * p1 + cw[2:3] * gate
        act = (conv * _sigmoid(conv)) * ups[s]
        acc_ref[r, :] += jnp.dot(act.astype(BF16), wdn_ref[...], preferred_element_type=F32)
    if carry:
        halo_ref[c] = prev
        cs_ref[0, 0] = prev

    @pl.when(c == N_FF_CHUNKS - 1)
    def _():
        y_ref[0] = acc_ref[...]


def _ffn_weights(gain, w_up, conv_w, conv_b, w_down):
    cw = jnp.concatenate([conv_w, conv_b[None], jnp.zeros((4, D_FF), F32)], axis=0)
    return gain.reshape(1, D_MODEL), w_up.astype(BF16), cw, w_down.astype(BF16)


def _ffn_common_specs():
    return [pl.BlockSpec((1, D_MODEL), lambda b, i, c: (0, 0)),
            pl.BlockSpec((D_MODEL, FF_CHUNK), lambda b, i, c: (0, c)),
            pl.BlockSpec((D_MODEL, FF_CHUNK), lambda b, i, c: (0, N_FF_CHUNKS + c)),
            pl.BlockSpec((8, FF_CHUNK), lambda b, i, c: (0, c)),
            pl.BlockSpec((FF_CHUNK, D_MODEL), lambda b, i, c: (c, 0))]


def _ffn_prompt(x, weights, *, batch, seq):
    tm = min(1024, seq)
    x_spec = pl.BlockSpec((1, tm, D_MODEL), lambda b, i, c: (b, i, 0))
    y, cs = pl.pallas_call(
        functools.partial(_ffn_kernel, tm=tm, carry=True, period=0),
        grid=(batch, seq // tm, N_FF_CHUNKS),
        in_specs=[x_spec] + _ffn_common_specs(),
        out_specs=(x_spec, pl.BlockSpec((1, 1, 8, FF_CHUNK), lambda b, i, c: (b * (seq // tm) + i, c, 0, 0))),
        out_shape=(jax.ShapeDtypeStruct((batch, seq, D_MODEL), F32),
                   jax.ShapeDtypeStruct((batch * (seq // tm), N_FF_CHUNKS, 8, FF_CHUNK), F32)),
        scratch_shapes=[pltpu.VMEM((tm, D_MODEL), BF16), pltpu.VMEM((tm, D_MODEL), F32),
                        pltpu.VMEM((N_FF_CHUNKS, 8, FF_CHUNK), F32)],
        compiler_params=_cparams(("arbitrary", "arbitrary", "arbitrary")),
        name="ffn_prompt",
    )(x, weights[0], weights[1], *weights[1:])
    cs = cs.reshape(batch, seq // tm, N_FF_CHUNKS, 8, FF_CHUNK)[:, -1]
    conv_state = cs[:, :, 6:8, :].transpose(0, 2, 1, 3).reshape(batch, 2, D_FF)
    return y, conv_state


def _ffn_sample(x, conv_buf, weights, *, batch, seq):
    n = batch * seq
    x_spec = pl.BlockSpec((1, n, D_MODEL), lambda b, i, c: (0, 0, 0))
    col_spec = pl.BlockSpec((1, n, FF_CHUNK), lambda b, i, c: (0, 0, c))
    buf_spec = pl.BlockSpec((batch, FF_CHUNK), lambda b, i, c: (0, c))
    y, gate = pl.pallas_call(
        functools.partial(_ffn_kernel, tm=n, carry=False, period=seq),
        grid=(1, 1, N_FF_CHUNKS),
        in_specs=[x_spec] + _ffn_common_specs() + [buf_spec, buf_spec],
        out_specs=(x_spec, col_spec),
        out_shape=(jax.ShapeDtypeStruct((1, n, D_MODEL), F32), jax.ShapeDtypeStruct((1, n, D_FF), F32)),
        scratch_shapes=[pltpu.VMEM((n, D_MODEL), BF16), pltpu.VMEM((n, D_MODEL), F32)],
        compiler_params=_cparams(("arbitrary", "arbitrary", "arbitrary")),
        name="ffn_sample",
    )(x[None], weights[0], weights[1], *weights[1:], conv_buf[:, 0], conv_buf[:, 1])
    return y[0], gate.reshape(batch, seq, D_FF)[:, seq - 2:]


def kernel(x_prompt, x_sample, cache_moba_k, cache_moba_v, state_s5_re, state_s5_im, cache_diff_k, cache_diff_v, state_ffn_conv, page_table, rel_bias, ab_norm, w_ab_in, w_ab_out, moba_q_gain, moba_k_gain, s5_a_re, s5_a_im, s5_log_dt, s5_b_re, s5_b_im, s5_c_re, s5_c_im, s5_d, s5_w_glu, c_norm, w_c_in, w_c_out, diff_q_gain, diff_k_gain, diff_lq1, diff_lk1, diff_lq2, diff_lk2, diff_head_gain, ffn_norm, w_ffn_up, ffn_conv_w, ffn_conv_b, w_ffn_down):
    batch, seq, _ = x_prompt.shape
    dbatch, dseq, _ = x_sample.shape
    depth = ffn_norm.shape[0]
    n_pages = page_table.shape[1]
    past_len = n_pages * PAGE_SIZE
    n_phys = cache_moba_k.shape[0]

    bias_tiles, bias_rows = _build_bias(rel_bias, past_len, dseq)
    gmat = _group_matrix()
    moba_k_pages = cache_moba_k.transpose(0, 1, 3, 4, 2).reshape(n_phys, -1, MOBA_WIDTH, PAGE_SIZE)
    moba_v_pages = cache_moba_v.transpose(0, 1, 3, 4, 2).reshape(n_phys, -1, MOBA_WIDTH, PAGE_SIZE)
    diff_k_pages = cache_diff_k.transpose(0, 1, 3, 4, 5, 2).reshape(n_phys, -1, D_MODEL, PAGE_SIZE)
    diff_v_pages = cache_diff_v.reshape(n_phys, -1, PAGE_SIZE * DIFF_HEADS, 2 * HEAD_DIM)
    diff_bias_rows = jnp.broadcast_to(
        bias_rows.reshape(DIFF_HEADS, 1, dseq, -1), (DIFF_HEADS, 2, dseq, bias_rows.shape[-1])
    ).reshape(2 * DIFF_HEADS * dseq, -1)
    zero_state = jnp.zeros((batch, S5_GROUPS, S5_STATE), F32)
    n_ab, n_c = (depth + 1) // 2, depth // 2
    mk_all = jnp.zeros((batch, n_ab, MOBA_WIDTH, seq), F32)
    mv_all = jnp.zeros((batch, n_ab, MOBA_WIDTH, seq), F32)
    dk_all = jnp.zeros((batch, n_c, D_MODEL, seq), F32)
    dv_all = jnp.zeros((batch, n_c, seq, D_MODEL), F32)

    xp = x_prompt.reshape(batch * seq, D_MODEL)
    xs = x_sample.reshape(dbatch * dseq, D_MODEL)
    mk_s, mv_s = [], []
    sr_p, sr_s, si_p, si_s = [], [], [], []
    dk_s, dv_s = [], []
    cb_p, cb_s = [], []
    row = lambda v: v.reshape(1, -1)
    for layer in range(depth):
        li = layer // 2
        if layer % 2 == 0:
            gain = row(ab_norm[li])
            w_in = w_ab_in[li].astype(BF16)
            w_out = w_ab_out[li].astype(BF16)
            w_glu = s5_w_glu[li].astype(BF16)
            q_gain = row(jnp.tile(moba_q_gain[li], MOBA_WIDTH // HEAD_DIM))
            k_gain = row(jnp.tile(moba_k_gain[li], MOBA_WIDTH // HEAD_DIM))
            s5_args = (s5_a_re[li], s5_a_im[li], s5_log_dt[li], s5_b_re[li], s5_b_im[li],
                       s5_c_re[li], s5_c_im[li], s5_d[li])
            qa, qb, kb, mk_all, mv_all, vt, u, kmean = _in_proj(
                xp, gain, w_in, q_gain, k_gain, gmat, width=MOBA_WIDTH, has_u=True, seq=seq,
                rows_all=(mk_all, mv_all), li=li)
            shp = lambda a: a.reshape(batch, seq, MOBA_WIDTH)
            o_a = _attn_prompt("moba", shp(qa), shp(qb), shp(kb), vt, bias_tiles,
                               (kmean.reshape(batch, seq // MOBA_BLOCK, MOBA_WIDTH),), batch=batch)
            tabs = _s5_tables(*s5_args, chunk=16)
            y, s_re, s_im = _s5(u, tabs, zero_state, zero_state, batch=batch, seq=seq)
            xp = _ab_out(xp, o_a.reshape(batch * seq, MOBA_WIDTH), y, w_glu, w_out)
            sr_p.append(s_re); si_p.append(s_im)
            q, k, v, u = _in_proj(xs, gain, w_in, q_gain, k_gain, gmat, width=MOBA_WIDTH, has_u=True)
            shs = lambda a: a.reshape(dbatch, dseq, MOBA_WIDTH)
            o_a = _sample_attn("moba", page_table, shs(q), shs(k), shs(v), bias_rows,
                               moba_k_pages, moba_v_pages, li)
            tabs = _s5_tables(*s5_args, chunk=dseq)
            y, s_re, s_im = _s5(u, tabs, state_s5_re[li], state_s5_im[li], batch=dbatch, seq=dseq)
            xs = _ab_out(xs, o_a.reshape(dbatch * dseq, MOBA_WIDTH), y, w_glu, w_out)
            mk_s.append(k.reshape(dbatch, dseq, 8, HEAD_DIM)); mv_s.append(v.reshape(dbatch, dseq, 8, HEAD_DIM))
            sr_s.append(s_re); si_s.append(s_im)
        else:
            lam_init = 0.8 - 0.6 * math.exp(-0.3 * layer)
            gain = row(c_norm[li])
            w_in = w_c_in[li].astype(BF16)
            w_out = w_c_out[li].astype(BF16)
            q_gain = row(jnp.tile(diff_q_gain[li], D_MODEL // HEAD_DIM))
            k_gain = row(jnp.tile(diff_k_gain[li], D_MODEL // HEAD_DIM))
            lams = (row(diff_lq1[li]), row(diff_lk1[li]), row(diff_lq2[li]), row(diff_lk2[li]))
            head_gain = row(diff_head_gain[li])
            qa, qb, kb, dk_all, dv_all, vt = _in_proj(
                xp, gain, w_in, q_gain, k_gain, gmat, width=D_MODEL, has_u=False, seq=seq,
                rows_all=(dk_all, dv_all), li=li)
            shp = lambda a: a.reshape(batch, seq, D_MODEL)
            o = _attn_prompt("diff", shp(qa), shp(qb), shp(kb), vt, bias_tiles,
                             lams + (head_gain,), batch=batch, lam_init=lam_init)
            xp = _c_out(xp, o.reshape(batch * seq, D_MODEL), w_out)
            q, k, v = _in_proj(xs, gain, w_in, q_gain, k_gain, gmat, width=D_MODEL, has_u=False)
            shs = lambda a: a.reshape(dbatch, dseq, D_MODEL)
            o = _sample_attn("diff", page_table, shs(q), shs(k), shs(v), diff_bias_rows,
                             diff_k_pages, diff_v_pages, li, extra=lams + (head_gain,),
                             lam_init=lam_init)
            xs = _c_out(xs, o.reshape(dbatch * dseq, D_MODEL), w_out)
            dk_s.append(k.reshape(dbatch, dseq, 8, 2, HEAD_DIM)); dv_s.append(v.reshape(dbatch, dseq, 8, 2 * HEAD_DIM))
        weights = _ffn_weights(ffn_norm[layer], w_ffn_up[layer], ffn_conv_w[layer], ffn_conv_b[layer],
                               w_ffn_down[layer])
        yp, buf_p = _ffn_prompt(xp.reshape(batch, seq, D_MODEL), weights, batch=batch, seq=seq)
        xp = yp.reshape(batch * seq, D_MODEL)
        xs, buf_s = _ffn_sample(xs, state_ffn_conv[layer], weights, batch=dbatch, seq=dseq)
        cb_p.append(buf_p); cb_s.append(buf_s)

    st = lambda xs_, ax: jnp.stack(xs_, axis=ax)
    mk_p = mk_all.reshape(batch, n_ab, 8, HEAD_DIM, seq).transpose(0, 1, 4, 2, 3)
    mv_p = mv_all.reshape(batch, n_ab, 8, HEAD_DIM, seq).transpose(0, 1, 4, 2, 3)
    dk_p = dk_all.reshape(batch, n_c, 8, 2, HEAD_DIM, seq).transpose(0, 1, 5, 2, 3, 4)
    dv_p = dv_all.reshape(batch, n_c, seq, 8, 2 * HEAD_DIM)
    return (xp.reshape(batch, seq, D_MODEL), xs.reshape(dbatch, dseq, D_MODEL),
            mk_p, st(mk_s, 1), mv_p, st(mv_s, 1),
            st(sr_p, 0), st(sr_s, 0), st(si_p, 0), st(si_s, 0),
            dk_p, st(dk_s, 1), dv_p, st(dv_s, 1),
            st(cb_p, 0), st(cb_s, 0))
```

```python
import functools
import math

import jax
import jax.numpy as jnp
from jax import lax
from jax.experimental import pallas as pl
from jax.experimental.pallas import tpu as pltpu

F32 = jnp.float32
BF16 = jnp.bfloat16

D_MODEL = 1024
HEAD_DIM = 64
MOBA_WIDTH = 512
MOBA_BLOCK = 256
MOBA_TOPK = 3
S5_WIDTH = 512
S5_GROUP = 16
S5_GROUPS = 32
S5_STATE = 64
DIFF_HEADS = 8
REL_BUCKETS = 32
REL_MAX_DIST = 128
D_FF = 2816
RMS_EPS = 1e-6
PAGE_SIZE = 128

ATT_TILE = 256
FF_CHUNK = 256
N_FF_CHUNKS = D_FF // FF_CHUNK
M_INIT = -1e30
LOG2E = math.log2(math.e)
VT_ROWS = 128 + 16
VMEM_LIMIT = 56 * 1024 * 1024

_NT = (((1,), (1,)), ((), ()))


def _log2(n):
    assert n & (n - 1) == 0
    return n.bit_length() - 1


def _cparams(sem, vmem=VMEM_LIMIT):
    return pltpu.CompilerParams(dimension_semantics=sem, vmem_limit_bytes=vmem)


def _split(a):
    hi = a.astype(BF16)
    lo = (a - hi.astype(F32)).astype(BF16)
    return hi, lo


def _dot3(a, b):
    ah, al = _split(a)
    bh, bl = _split(b)
    d = functools.partial(jnp.dot, preferred_element_type=F32)
    return d(ah, bh) + d(ah, bl) + d(al, bh)


def _dot3_nt(a, b):
    ah, al = _split(a)
    bh, bl = _split(b)
    d = functools.partial(lax.dot_general, dimension_numbers=_NT, preferred_element_type=F32)
    return d(ah, bh) + d(ah, bl) + d(al, bh)


def _rms_rows(x, g):
    ms = jnp.mean(x * x, axis=-1, keepdims=True)
    return x * lax.rsqrt(ms + RMS_EPS) * g


def _group_mean_sq(t, gmat):
    t2 = t * t
    hi, lo = _split(t2)
    return (jnp.dot(hi, gmat, preferred_element_type=F32)
            + jnp.dot(lo, gmat, preferred_element_type=F32))


def _head_norm(t, gmat, gain):
    slab = gmat.shape[0]
    parts = [_group_mean_sq(t[:, s:s + slab], gmat) for s in range(0, t.shape[-1], slab)]
    ms = jnp.concatenate(parts, axis=1)
    return t * lax.rsqrt(ms + RMS_EPS) * gain


def _sigmoid(x):
    return 1.0 / (1.0 + jnp.exp(-x))


def _rel_bucket(d):
    dist = jnp.maximum(d, 0)
    exact = REL_BUCKETS // 2
    log_ratio = jnp.log(jnp.maximum(dist, 1).astype(F32) / exact) / math.log(REL_MAX_DIST / exact)
    large = exact + (log_ratio * (REL_BUCKETS - exact)).astype(jnp.int32)
    return jnp.where(dist < exact, dist, jnp.minimum(large, REL_BUCKETS - 1))


def _bias_lookup(bucket, tab_ref, h):
    out = jnp.zeros(bucket.shape, F32)
    for b in range(REL_BUCKETS):
        out = jnp.where(bucket == b, tab_ref[b, h], out)
    return out


def _bias_tiles_kernel(tab_ref, o_ref):
    h = pl.program_id(0)
    key = lax.broadcasted_iota(jnp.int32, (ATT_TILE, ATT_TILE), 0)
    qry = lax.broadcasted_iota(jnp.int32, (ATT_TILE, ATT_TILE), 1)
    d0 = qry - key
    far = tab_ref[REL_BUCKETS - 1, h]
    near0 = (_bias_lookup(_rel_bucket(d0), tab_ref, h) - far) * LOG2E
    o_ref[0, 0] = jnp.where(d0 >= 0, near0, -jnp.inf)
    o_ref[0, 1] = (_bias_lookup(_rel_bucket(d0 + ATT_TILE), tab_ref, h) - far) * LOG2E


def _bias_sample_kernel(tab_ref, o_ref, *, past_len, n_new):
    h = pl.program_id(0)
    shape = o_ref.shape[1:]
    i = lax.broadcasted_iota(jnp.int32, shape, 0)
    k = lax.broadcasted_iota(jnp.int32, shape, 1)
    d = past_len + i - k
    ok = (d >= 0) & (k < past_len + n_new)
    o_ref[0] = jnp.where(ok, _bias_lookup(_rel_bucket(d), tab_ref, h), -jnp.inf)


def _build_bias(rel_bias, past_len, n_new):
    n_heads = rel_bias.shape[1]
    smem = pl.BlockSpec(memory_space=pltpu.SMEM)
    tiles = pl.pallas_call(
        _bias_tiles_kernel,
        grid=(n_heads,),
        in_specs=[smem],
        out_specs=pl.BlockSpec((1, 2, ATT_TILE, ATT_TILE), lambda h: (h, 0, 0, 0)),
        out_shape=jax.ShapeDtypeStruct((n_heads, 2, ATT_TILE, ATT_TILE), F32),
        compiler_params=_cparams(("arbitrary",)),
        name="bias_tiles",
    )(rel_bias)
    width = past_len + PAGE_SIZE
    sample = pl.pallas_call(
        functools.partial(_bias_sample_kernel, past_len=past_len, n_new=n_new),
        grid=(n_heads,),
        in_specs=[smem],
        out_specs=pl.BlockSpec((1, n_new, width), lambda h: (h, 0, 0)),
        out_shape=jax.ShapeDtypeStruct((n_heads, n_new, width), F32),
        compiler_params=_cparams(("arbitrary",)),
        name="bias_sample",
    )(rel_bias)
    return tiles, sample.reshape(n_heads * n_new, width)


def _first_half_mask(shape):
    lane = lax.broadcasted_iota(jnp.int32, shape, 1)
    return (lane & HEAD_DIM) == 0


def _in_proj_kernel(x_ref, g_ref, w_ref, qg_ref, kg_ref, gm_ref, *outs, tm, width, has_u, prompt):
    h = _rms_rows(x_ref[...], g_ref[...]).astype(BF16)
    y = jnp.dot(h, w_ref[...], preferred_element_type=F32)
    gm = gm_ref[...]
    qn = _head_norm(y[:, 0:width], gm, qg_ref[...]) * (HEAD_DIM ** -0.5)
    kn = _head_norm(y[:, width:2 * width], gm, kg_ref[...])
    v = y[:, 2 * width:3 * width]
    if not prompt:
        outs[0][...] = qn
        outs[1][...] = kn
        outs[2][...] = v
        if has_u:
            outs[3][...] = y[:, 3 * width:4 * width]
        return
    qa_ref, qb_ref, kb_ref, kt_ref, v_ref, vt_ref = outs[2:8]
    outs = outs[2:]
    first = _first_half_mask(qn.shape)
    ql = qn * LOG2E
    qa_ref[...] = jnp.where(first, ql, 0.0).astype(BF16)
    qb_ref[...] = jnp.where(first, 0.0, ql).astype(BF16)
    kb_ref[...] = kn.astype(BF16)
    kt_ref[0, 0] = kn.T
    v_t = v.T
    v_ref[0, 0] = v_t if has_u else v
    v_tb = v_t.astype(BF16)
    ones = jnp.ones((VT_ROWS - 128, ATT_TILE), BF16)
    for g in range(width // 128):
        for j in range(tm // ATT_TILE):
            vt_ref[0, g, j] = jnp.concatenate(
                [v_tb[g * 128:(g + 1) * 128, j * ATT_TILE:(j + 1) * ATT_TILE], ones], axis=0)
    if has_u:
        u_ref, km_ref = outs[6:]
        u_ref[...] = y[:, 3 * width:4 * width]
        for j in range(tm // MOBA_BLOCK):
            km_ref[j] = jnp.mean(kn[j * MOBA_BLOCK:(j + 1) * MOBA_BLOCK], axis=0, keepdims=True)


NORM_SLAB = 256


def _group_matrix():
    i = jnp.arange(NORM_SLAB) // HEAD_DIM
    return jnp.where(i[:, None] == i[None, :], 1.0 / HEAD_DIM, 0.0).astype(BF16)


def _row_spec(tm, width):
    return pl.BlockSpec((tm, width), lambda i: (i, 0))


def _const_spec(shape):
    nd = len(shape)
    return pl.BlockSpec(shape, lambda *_: (0,) * nd)


def _in_proj(x, gain, w_bf, q_gain, k_gain, gmat, *, width, has_u, seq=None, rows_all=None, li=0):
    n = x.shape[0]
    prompt = seq is not None
    tm = min(512, n)
    sds = jax.ShapeDtypeStruct
    n_out = 4 if has_u else 3
    in_specs = [_row_spec(tm, D_MODEL), _const_spec((1, D_MODEL)), _const_spec((D_MODEL, n_out * width)),
                _const_spec((1, width)), _const_spec((1, width)), _const_spec((NORM_SLAB, NORM_SLAB))]
    args = (x, gain, w_bf, q_gain, k_gain, gmat)
    aliases = {}
    if not prompt:
        out_shape = tuple(sds((n, width), F32) for _ in range(n_out))
        out_specs = tuple(_row_spec(tm, width) for _ in range(n_out))
    else:
        tps = seq // tm
        batch = n // seq
        groups = width // 128
        feat_spec = pl.BlockSpec((1, 1, width, tm), lambda i: (i // tps, li, 0, i % tps))
        tok_spec = pl.BlockSpec((1, 1, tm, width), lambda i: (i // tps, li, i % tps, 0))
        out_shape = [sds((n, width), BF16), sds((n, width), BF16), sds((n, width), BF16),
                     sds(rows_all[0].shape, F32), sds(rows_all[1].shape, F32),
                     sds((batch, groups, seq // ATT_TILE, VT_ROWS, ATT_TILE), BF16)]
        out_specs = [_row_spec(tm, width)] * 3 + [
            feat_spec, feat_spec if has_u else tok_spec,
            pl.BlockSpec((1, groups, tm // ATT_TILE, VT_ROWS, ATT_TILE),
                         lambda i: (i // tps, 0, i % tps, 0, 0))]
        if has_u:
            out_shape += [sds((n, width), F32), sds((n // MOBA_BLOCK, 1, width), F32)]
            out_specs += [_row_spec(tm, width),
                          pl.BlockSpec((tm // MOBA_BLOCK, 1, width), lambda i: (i, 0, 0))]
        in_specs += [pl.BlockSpec(memory_space=pl.ANY)] * 2
        args += tuple(rows_all)
        aliases = {6: 3, 7: 4}
    return pl.pallas_call(
        functools.partial(_in_proj_kernel, tm=tm, width=width, has_u=has_u, prompt=prompt),
        grid=(n // tm,),
        in_specs=in_specs,
        out_specs=tuple(out_specs),
        out_shape=tuple(out_shape),
        input_output_aliases=aliases,
        compiler_params=_cparams(("arbitrary",)),
        name="ab_in" if has_u else "c_in",
    )(*args)


ATT_Q_PER_STEP = 4


def _attn_prompt_kernel(*refs, mode, lam_init, q_per_step):
    def one_tile(j, carry):
        rows = pl.ds(pl.multiple_of(j * ATT_TILE, ATT_TILE), ATT_TILE)
        _attn_tile(refs, mode, lam_init, pl.program_id(2) * q_per_step + j, rows)
        return carry

    lax.fori_loop(0, q_per_step, one_tile, 0)


def _attn_tile(refs, mode, lam_init, qi, rows):
    if mode == "moba":
        qa_ref, qb_ref, k_ref, vt_ref, bias_ref, km_ref, o_ref, acc_ref, sel_ref = refs
    else:
        (qa_ref, qb_ref, k_ref, vt_ref, bias_ref, lq1_ref, lk1_ref, lq2_ref, lk2_ref,
         hg_ref, o_ref, acc_ref) = refs
    t = ATT_TILE
    q2 = jnp.concatenate([qa_ref[0, rows, :], qb_ref[0, rows, :]], axis=0)
    if mode == "moba":
        tile = lambda j: jnp.concatenate([bias_ref[0, j], bias_ref[1, j]], axis=1)
    else:
        tile = lambda j: jnp.concatenate([bias_ref[0, j], bias_ref[0, j]], axis=1)

    acc_ref[...] = jnp.zeros(acc_ref.shape, F32)

    if mode == "moba":
        km_hi, km_lo = _split(km_ref[0])
        gate = (lax.dot_general(km_hi, q2, _NT, preferred_element_type=F32)
                + lax.dot_general(km_lo, q2, _NT, preferred_element_type=F32))
        blk_i = lax.broadcasted_iota(jnp.int32, gate.shape, 0)
        blk_f = blk_i.astype(F32)
        valid = blk_i < qi
        cur = jnp.where(valid, gate, -jnp.inf)
        sel_pen = jnp.full(gate.shape, -jnp.inf, F32)
        for _ in range(MOBA_TOPK):
            top = jnp.max(cur, axis=0, keepdims=True)
            idx = jnp.min(jnp.where(cur == top, blk_f, 1e9), axis=0, keepdims=True)
            pick = blk_f == idx
            sel_pen = jnp.where(pick, jnp.where(valid, 0.0, -jnp.inf), sel_pen)
            cur = jnp.where(pick, -jnp.inf, cur)
        sel_ref[...] = sel_pen
        block_pen = lambda kj: sel_ref[pl.ds(kj, 1), :]
    else:
        block_pen = None

    def tiles(m_prev, kjs, bias_adds):
        scores = []
        for kj, bias_add in zip(kjs, bias_adds):
            start = pl.multiple_of(kj * t, t)
            s = lax.dot_general(k_ref[0, pl.ds(start, t), :], q2, _NT,
                                preferred_element_type=F32)
            scores.append(s if bias_add is None else s + bias_add)
        parts = []
        for s in scores:
            m_t = jnp.maximum(jnp.max(s, axis=0, keepdims=True), M_INIT)
            parts.append((m_t, jnp.exp2(s - m_t).astype(BF16)))
        m_new = m_prev
        for m_t, _ in parts:
            m_new = jnp.maximum(m_new, m_t)
        acc = jnp.exp2(m_prev - m_new) * acc_ref[...]
        for kj, (m_t, p) in zip(kjs, parts):
            acc = acc + jnp.exp2(m_t - m_new) * jnp.dot(vt_ref[0, 0, kj], p,
                                                        preferred_element_type=F32)
        acc_ref[...] = acc
        return m_new

    def far_pen(kj):
        return None if block_pen is None else block_pen(kj)

    def far_run(first, count, m_prev):
        kjs = [first + j for j in range(count)]
        return tiles(m_prev, kjs, [far_pen(kj) for kj in kjs])

    n_far = jnp.maximum(qi - 1, 0)
    m = jnp.full((1, 2 * t), M_INIT, F32)
    n_quads = lax.shift_right_logical(n_far, 2)
    m = lax.fori_loop(0, n_quads, lambda i, c: far_run(4 * i, 4, c), m)

    def last_group(n_left):
        def run(m_prev):
            kn = jnp.maximum(qi - 1, 0)
            near = tile(1) + jnp.where(qi >= 1, 0.0, -jnp.inf)
            if block_pen is not None:
                near = near + block_pen(kn)
            left = [4 * n_quads + j for j in range(n_left)]
            return tiles(m_prev, left + [kn, qi], [far_pen(kj) for kj in left] + [near, tile(0)])
        return run

    lax.switch(n_far & 3, [last_group(n) for n in range(4)], m)

    out = acc_ref[0:128, :] / acc_ref[128:129, :]
    if mode == "moba":
        row = lax.broadcasted_iota(jnp.int32, (128, t), 0)
        o_ref[0, rows, :] = jnp.where(row < HEAD_DIM, out[:, :t], out[:, t:]).T
    else:
        l1 = jnp.sum(lq1_ref[...] * lk1_ref[...], axis=1, keepdims=True)
        l2 = jnp.sum(lq2_ref[...] * lk2_ref[...], axis=1, keepdims=True)
        lam = jnp.exp(l1) - jnp.exp(l2) + lam_init
        o = (out[:, :t] - lam * out[:, t:]).T
        o_ref[0, rows, :] = _rms_rows(o, hg_ref[...]) * (1.0 - lam_init)


def _attn_prompt(mode, qa, qb, kb, vt, bias_tiles, extra, *, batch, lam_init=0.0):
    _, seq, width = qa.shape
    groups = width // 128
    nq = seq // ATT_TILE
    q_per_step = math.gcd(nq, ATT_Q_PER_STEP)
    q_spec = pl.BlockSpec((1, q_per_step * ATT_TILE, 128), lambda b, g, i: (b, i, g))
    k_spec = pl.BlockSpec((1, seq, 128), lambda b, g, i: (b, 0, g))
    vt_spec = pl.BlockSpec((1, 1, nq, VT_ROWS, ATT_TILE), lambda b, g, i: (b, g, 0, 0, 0))
    scratch = [pltpu.VMEM((VT_ROWS, 2 * ATT_TILE), F32)]
    if mode == "moba":
        (kmean,) = extra
        nb = kmean.shape[1]
        assert nb % 8 == 0
        bias_spec = pl.BlockSpec((2, 2, ATT_TILE, ATT_TILE), lambda b, g, i: (g, 0, 0, 0))
        extra_specs = [pl.BlockSpec((1, nb, 128), lambda b, g, i: (b, 0, g))]
        scratch.append(pltpu.VMEM((nb, 2 * ATT_TILE), F32))
    else:
        bias_spec = pl.BlockSpec((1, 2, ATT_TILE, ATT_TILE), lambda b, g, i: (g, 0, 0, 0))
        extra_specs = [_const_spec((1, HEAD_DIM))] * 4 + [_const_spec((1, 128))]
    return pl.pallas_call(
        functools.partial(_attn_prompt_kernel, mode=mode, lam_init=lam_init, q_per_step=q_per_step),
        grid=(batch, groups, nq // q_per_step),
        in_specs=[q_spec, q_spec, k_spec, vt_spec, bias_spec] + extra_specs,
        out_specs=q_spec,
        out_shape=jax.ShapeDtypeStruct((batch, seq, width), F32),
        scratch_shapes=scratch,
        compiler_params=_cparams(("arbitrary", "arbitrary", "arbitrary")),
        name=f"{mode}_prompt",
    )(qa, qb, kb, vt, bias_tiles, *extra)


def _page_map(b, pt_ref, *, j, li):
    return (pt_ref[b, j], li, 0, 0)


def _pad_rows(x, rows):
    return jnp.concatenate([x, jnp.zeros((rows - x.shape[0], x.shape[1]), x.dtype)], axis=0)


def _softmax_pv(scores, values, transposed):
    mx = scores[0]
    for s in scores[1:]:
        mx = jnp.maximum(mx, s)
    m = jnp.max(mx, axis=1, keepdims=True)
    tot = None
    acc = None
    for s, v, tr in zip(scores, values, transposed):
        p = jnp.exp(s - m)
        tot = p if tot is None else tot + p
        if tr:
            d = lax.dot_general(p.astype(BF16), v, _NT, preferred_element_type=F32)
        else:
            d = jnp.dot(p.astype(BF16), v, preferred_element_type=F32)
        acc = d if acc is None else acc + d
    return acc / jnp.sum(tot, axis=1, keepdims=True)


def _moba_sample_kernel(pt_ref, q_ref, kn_ref, vn_ref, bias_ref, *rest, n_pages):
    kp = rest[:n_pages]
    vp = rest[n_pages:2 * n_pages]
    o_ref = rest[2 * n_pages]
    nq = q_ref.shape[1]
    heads = MOBA_WIDTH // HEAD_DIM
    rows = heads * nq
    q = q_ref[0]
    q_rep = jnp.concatenate([q] * heads, axis=0)
    row = lax.broadcasted_iota(jnp.int32, (rows, MOBA_WIDTH), 0)
    lane = lax.broadcasted_iota(jnp.int32, (rows, MOBA_WIDTH), 1)
    own_head = (row >> _log2(nq)) == (lane >> _log2(HEAD_DIM))
    qbd = jnp.where(own_head, q_rep, 0.0).astype(BF16)

    pages_per_block = MOBA_BLOCK // PAGE_SIZE
    n_blocks = n_pages // pages_per_block
    scores = [jnp.dot(qbd, kp[j][0, 0].astype(BF16), preferred_element_type=F32)
              for j in range(n_pages)]
    lane_i = lax.broadcasted_iota(jnp.int32, (rows, 128), 1)
    gate = jnp.zeros((rows, 128), F32)
    for n in range(n_blocks):
        blk = scores[n * pages_per_block]
        for j in range(n * pages_per_block + 1, (n + 1) * pages_per_block):
            blk = blk + scores[j]
        gate = jnp.where(lane_i == n, jnp.sum(blk, axis=1, keepdims=True) * (1.0 / MOBA_BLOCK), gate)
    lane_f = lane_i.astype(F32)
    valid = lane_i < n_blocks
    cur = jnp.where(valid, gate, -jnp.inf)
    sel_pen = jnp.full(gate.shape, -jnp.inf, F32)
    for _ in range(MOBA_TOPK):
        top = jnp.max(cur, axis=1, keepdims=True)
        idx = jnp.min(jnp.where(cur == top, lane_f, 1e9), axis=1, keepdims=True)
        pick = lane_f == idx
        sel_pen = jnp.where(pick, jnp.where(valid, 0.0, -jnp.inf), sel_pen)
        cur = jnp.where(pick, -jnp.inf, cur)

    for j in range(n_pages):
        n = j // pages_per_block
        scores[j] = scores[j] + bias_ref[:, j * PAGE_SIZE:(j + 1) * PAGE_SIZE] + sel_pen[:, n:n + 1]
    values = [vp[j][0, 0].astype(BF16) for j in range(n_pages)]
    k_new = _pad_rows(kn_ref[0], PAGE_SIZE).astype(BF16)
    scores.append(lax.dot_general(qbd, k_new, _NT, preferred_element_type=F32)
                  + bias_ref[:, n_pages * PAGE_SIZE:(n_pages + 1) * PAGE_SIZE])
    values.append(_pad_rows(vn_ref[0], PAGE_SIZE).astype(BF16))

    res = _softmax_pv(scores, values, [True] * n_pages + [False])
    res = jnp.where(own_head, res, 0.0)
    out = res[0:nq]
    for h in range(1, heads):
        out = out + res[h * nq:(h + 1) * nq]
    o_ref[0] = out


def _diff_sample_kernel(pt_ref, q_ref, kn_ref, vn_ref, bias_ref, lq1_ref, lk1_ref, lq2_ref, lk2_ref,
                        hg_ref, *rest, n_pages, lam_init):
    kp = rest[:n_pages]
    vp = rest[n_pages:2 * n_pages]
    o_ref = rest[2 * n_pages]
    nq = q_ref.shape[1]
    heads = DIFF_HEADS
    hw = 2 * HEAD_DIM
    l1 = jnp.sum(lq1_ref[...] * lk1_ref[...], axis=1, keepdims=True)
    l2 = jnp.sum(lq2_ref[...] * lk2_ref[...], axis=1, keepdims=True)
    lam = jnp.exp(l1) - jnp.exp(l2) + lam_init
    lane = lax.broadcasted_iota(jnp.int32, (nq, hw), 1)
    new_cols = slice(n_pages * PAGE_SIZE, (n_pages + 1) * PAGE_SIZE)
    for h in range(heads):
        cols = slice(h * hw, (h + 1) * hw)
        qh = q_ref[0, :, cols]
        q2 = jnp.concatenate([jnp.where(lane < HEAD_DIM, qh, 0.0),
                              jnp.where(lane < HEAD_DIM, 0.0, qh)], axis=0).astype(BF16)
        bias = bias_ref[2 * nq * h:2 * nq * (h + 1), :]
        scores = []
        values = []
        for j in range(n_pages):
            kj = kp[j][0, 0, cols, :].astype(BF16)
            scores.append(jnp.dot(q2, kj, preferred_element_type=F32)
                          + bias[:, j * PAGE_SIZE:(j + 1) * PAGE_SIZE])
            values.append(vp[j][0, 0, pl.ds(h, PAGE_SIZE, stride=heads), :].astype(BF16))
        k_new = _pad_rows(kn_ref[0, :, cols], PAGE_SIZE).astype(BF16)
        scores.append(lax.dot_general(q2, k_new, _NT, preferred_element_type=F32) + bias[:, new_cols])
        values.append(_pad_rows(vn_ref[0, :, cols], PAGE_SIZE).astype(BF16))
        res = _softmax_pv(scores, values, [False] * (n_pages + 1))
        d = res[:nq] - lam * res[nq:]
        o_ref[0, :, cols] = _rms_rows(d, hg_ref[...]) * (1.0 - lam_init)


def _sample_attn(mode, page_table, q, k_new, v_new, bias_rows, cache_k, cache_v, li, extra=(),
                 lam_init=0.0):
    batch, n_new, width = q.shape
    n_pages = page_table.shape[1]
    tok_spec = pl.BlockSpec((1, n_new, width), lambda b, pt: (b, 0, 0))
    const = lambda shape: pl.BlockSpec(shape, lambda b, pt: (0,) * len(shape))
    page_specs = [pl.BlockSpec((1, 1, width, PAGE_SIZE), functools.partial(_page_map, j=j, li=li))
                  for j in range(n_pages)]
    if mode == "moba":
        body = functools.partial(_moba_sample_kernel, n_pages=n_pages)
        extra_specs = []
    else:
        body = functools.partial(_diff_sample_kernel, n_pages=n_pages, lam_init=lam_init)
        extra_specs = [const((1, HEAD_DIM))] * 4 + [const((1, 2 * HEAD_DIM))]
    grid_spec = pltpu.PrefetchScalarGridSpec(
        num_scalar_prefetch=1,
        grid=(batch,),
        in_specs=[tok_spec, tok_spec, tok_spec, const(bias_rows.shape)] + extra_specs
                 + page_specs + page_specs,
        out_specs=tok_spec,
    )
    return pl.pallas_call(
        body,
        grid_spec=grid_spec,
        out_shape=jax.ShapeDtypeStruct((batch, n_new, width), F32),
        compiler_params=_cparams(("arbitrary",)),
        name=f"{mode}_sample",
    )(page_table, q, k_new, v_new, bias_rows, *extra, *([cache_k] * n_pages), *([cache_v] * n_pages))


def _s5_tables_kernel(are_ref, aim_ref, ldt_ref, btre_ref, btim_ref, cre_ref, cim_ref,
                      xre_ref, xim_ref, zre_ref, zim_ref, wre_ref, wim_ref, vre_ref, vim_ref,
                      lre_ref, lim_ref, *, chunk):
    a_re = are_ref[...]
    a_im = aim_ref[...]
    dt = jnp.exp(ldt_ref[...])
    mag = jnp.exp(a_re * dt)
    lb_re = mag * jnp.cos(a_im * dt)
    lb_im = mag * jnp.sin(a_im * dt)
    den = a_re * a_re + a_im * a_im
    f_re = ((lb_re - 1.0) * a_re + lb_im * a_im) / den
    f_im = (lb_im * a_re - (lb_re - 1.0) * a_im) / den
    bb_re = f_re * btre_ref[...] - f_im * btim_ref[...]
    bb_im = f_re * btim_ref[...] + f_im * btre_ref[...]
    c_re = cre_ref[...]
    c_im = cim_ref[...]
    inv_mag = jnp.exp(-(a_re * dt))
    il_re = inv_mag * jnp.cos(a_im * dt)
    il_im = -inv_mag * jnp.sin(a_im * dt)

    def cmul(x_re, x_im, y_re, y_im):
        return x_re * y_re - x_im * y_im, x_re * y_im + x_im * y_re

    def put(ref, j, val):
        ref[:, j * S5_GROUP:(j + 1) * S5_GROUP, :] = val.reshape(S5_GROUPS, S5_GROUP, S5_STATE)

    one = jnp.ones_like(a_re)
    zero = jnp.zeros_like(a_re)
    pos = [(one, zero)]
    neg = [(one, zero)]
    for _ in range(chunk):
        pos.append(cmul(pos[-1][0], pos[-1][1], lb_re, lb_im))
        neg.append(cmul(neg[-1][0], neg[-1][1], il_re, il_im))
    for j in range(chunk):
        x = cmul(bb_re, bb_im, *neg[j])
        z = cmul(c_re, c_im, *pos[j])
        w = cmul(bb_re, bb_im, *pos[chunk - 1 - j])
        v = cmul(c_re, c_im, *pos[j + 1])
        put(xre_ref, j, x[0]); put(xim_ref, j, x[1])
        put(zre_ref, j, z[0]); put(zim_ref, j, z[1])
        put(wre_ref, j, w[0]); put(wim_ref, j, w[1])
        put(vre_ref, j, v[0]); put(vim_ref, j, -v[1])
    lre_ref[...] = pos[chunk][0].reshape(S5_GROUPS, S5_GROUP, S5_STATE)[:, 0:1, :]
    lim_ref[...] = pos[chunk][1].reshape(S5_GROUPS, S5_GROUP, S5_STATE)[:, 0:1, :]


def _s5_toeplitz_kernel(xre_ref, xim_ref, zre_ref, zim_ref, m_ref):
    m = _dot3_nt(xre_ref[0], zre_ref[0]) - _dot3_nt(xim_ref[0], zim_ref[0])
    r = lax.broadcasted_iota(jnp.int32, m.shape, 0) >> _log2(S5_GROUP)
    c = lax.broadcasted_iota(jnp.int32, m.shape, 1) >> _log2(S5_GROUP)
    m_ref[0] = jnp.where(c >= r, m, 0.0)


def _s5_tables(a_re, a_im, log_dt, b_re, b_im, c_re, c_im, d_skip, chunk):
    g, n, p = S5_GROUPS, S5_STATE, S5_GROUP
    kl = chunk * p
    rep = lambda x: jnp.repeat(x, p, axis=0)
    flat = lambda x: x.reshape(g * p, n)
    ins = (rep(a_re), rep(a_im), rep(jnp.broadcast_to(log_dt[:, None], (g, n))),
           flat(b_re.transpose(0, 2, 1)), flat(b_im.transpose(0, 2, 1)), flat(c_re), flat(c_im))
    big = jax.ShapeDtypeStruct((g, kl, n), F32)
    small = jax.ShapeDtypeStruct((g, 1, n), F32)
    outs = pl.pallas_call(
        functools.partial(_s5_tables_kernel, chunk=chunk),
        out_shape=(big,) * 8 + (small, small),
        compiler_params=pltpu.CompilerParams(vmem_limit_bytes=VMEM_LIMIT),
        name="s5_tables",
    )(*ins)
    x_re, x_im, z_re, z_im, w_re, w_im, v_re, v_im, l_re, l_im = outs
    gspec = pl.BlockSpec((1, kl, n), lambda i: (i, 0, 0))
    toep = pl.pallas_call(
        _s5_toeplitz_kernel,
        grid=(g,),
        in_specs=[gspec] * 4,
        out_specs=pl.BlockSpec((1, kl, kl), lambda i: (i, 0, 0)),
        out_shape=jax.ShapeDtypeStruct((g, kl, kl), F32),
        compiler_params=_cparams(("arbitrary",)),
        name="s5_toeplitz",
    )(x_re, x_im, z_re, z_im)

    def pad_lo(x):
        return jnp.pad(x, [(0, 0)] * (x.ndim - 1) + [(0, n)])

    def pad_hi(x):
        return jnp.pad(x, [(0, 0)] * (x.ndim - 1) + [(n, 0)])

    def pair_rows(w):
        w = w.reshape(g // 2, 2, kl, n)
        return jnp.concatenate([pad_lo(w[:, 0]), pad_hi(w[:, 1])], axis=1)

    def pair_pad(v):
        v = v.reshape(g // 2, 2, kl, n)
        return jnp.stack([pad_lo(v[:, 0]), pad_hi(v[:, 1])], axis=1).reshape(g, kl, 2 * n)

    def pair_lanes(l):
        l = l.reshape(g // 2, 2, 1, n)
        return jnp.concatenate([l[:, 0], l[:, 1]], axis=-1)

    d_tile = jnp.tile(d_skip, (1, chunk)).reshape(g, 1, kl)
    return dict(chunk=chunk, toep=toep, w_re=pair_rows(w_re), w_im=pair_rows(w_im),
                v_re=pair_pad(v_re), v_im=pair_pad(v_im), d=d_tile,
                l_re=pair_lanes(l_re), l_im=pair_lanes(l_im))


S5_GROUPS_PER_STEP = 4


def _s5_apply_kernel(u_ref, m_ref, wre_ref, wim_ref, vre_ref, vim_ref, d_ref, lre_ref, lim_ref,
                     s0re_ref, s0im_ref, y_ref, stre_ref, stim_ref, ere_ref, eim_ref, sre_ref, sim_ref,
                     *, n_chunks, batch):
    pairs = S5_GROUPS_PER_STEP // 2
    for k in range(pairs):
        ucat = jnp.concatenate([u_ref[2 * k], u_ref[2 * k + 1]], axis=1)
        ere_ref[k] = _dot3(ucat, wre_ref[k])
        eim_ref[k] = _dot3(ucat, wim_ref[k])
    l_re = lre_ref[...]
    l_im = lim_ref[...]

    def step(c, carry):
        s_re, s_im = carry
        rows = pl.ds(c * batch, batch)
        sre_ref[:, rows, :] = s_re
        sim_ref[:, rows, :] = s_im
        n_re = l_re * s_re - l_im * s_im + ere_ref[:, rows, :]
        n_im = l_re * s_im + l_im * s_re + eim_ref[:, rows, :]
        return n_re, n_im

    s_re, s_im = lax.fori_loop(0, n_chunks, step, (s0re_ref[...], s0im_ref[...]))
    stre_ref[...] = s_re
    stim_ref[...] = s_im
    for g in range(S5_GROUPS_PER_STEP):
        u = u_ref[g]
        y_ref[g] = (_dot3(u, m_ref[g]) + _dot3_nt(sre_ref[g // 2], vre_ref[g])
                    + _dot3_nt(sim_ref[g // 2], vim_ref[g]) + d_ref[g] * u)


def _s5_apply(u_chunks, tabs, s0_re, s0_im, *, n_chunks, batch):
    g, rows, kl = u_chunks.shape
    gs = S5_GROUPS_PER_STEP
    ps = gs // 2
    gspec = lambda shape: pl.BlockSpec((gs,) + shape, lambda i: (i, 0, 0))
    pspec = lambda shape: pl.BlockSpec((ps,) + shape, lambda i: (i, 0, 0))
    return pl.pallas_call(
        functools.partial(_s5_apply_kernel, n_chunks=n_chunks, batch=batch),
        grid=(g // gs,),
        in_specs=[gspec((rows, kl)), gspec((kl, kl)), pspec((2 * kl, 128)), pspec((2 * kl, 128)),
                  gspec((kl, 128)), gspec((kl, 128)), gspec((1, kl)), pspec((1, 128)), pspec((1, 128)),
                  pspec((batch, 128)), pspec((batch, 128))],
        out_specs=(gspec((rows, kl)), pspec((batch, 128)), pspec((batch, 128))),
        out_shape=(jax.ShapeDtypeStruct((g, rows, kl), F32),
                   jax.ShapeDtypeStruct((g // 2, batch, 128), F32),
                   jax.ShapeDtypeStruct((g // 2, batch, 128), F32)),
        scratch_shapes=[pltpu.VMEM((ps, rows, 128), F32)] * 4,
        compiler_params=_cparams(("arbitrary",)),
        name="s5_apply",
    )(u_chunks, tabs["toep"], tabs["w_re"], tabs["w_im"], tabs["v_re"], tabs["v_im"], tabs["d"],
      tabs["l_re"], tabs["l_im"], s0_re, s0_im)


def _s5(u, tabs, s0_re, s0_im, *, batch, seq):
    chunk = tabs["chunk"]
    g, p, n = S5_GROUPS, S5_GROUP, S5_STATE
    n_chunks = seq // chunk
    uc = u.reshape(batch, n_chunks, chunk, g, p).transpose(3, 1, 0, 2, 4)
    uc = uc.reshape(g, n_chunks * batch, chunk * p)
    pack = lambda s: s.reshape(batch, g // 2, 2 * n).transpose(1, 0, 2)
    y, st_re, st_im = _s5_apply(uc, tabs, pack(s0_re), pack(s0_im), n_chunks=n_chunks, batch=batch)
    y = y.reshape(g, n_chunks, batch, chunk, p).transpose(2, 1, 3, 0, 4).reshape(batch * seq, g * p)
    unpack = lambda s: s.transpose(1, 0, 2).reshape(batch, g, n)
    return y, unpack(st_re), unpack(st_im)


def _gelu_tanh(x):
    return 0.5 * x * (1.0 + jnp.tanh(math.sqrt(2.0 / math.pi) * (x + 0.044715 * (x * x * x))))


def _ab_out_kernel(x_ref, oa_ref, y_ref, wglu_ref, w_ref, o_ref):
    gl = _gelu_tanh(y_ref[...])
    gate = jnp.dot(gl.astype(BF16), wglu_ref[...], preferred_element_type=F32)
    ob = gl * _sigmoid(gate)
    cat = jnp.concatenate([oa_ref[...].astype(BF16), ob.astype(BF16)], axis=1)
    o_ref[...] = x_ref[...] + jnp.dot(cat, w_ref[...], preferred_element_type=F32)


def _c_out_kernel(x_ref, o_in_ref, w_ref, o_ref):
    o_ref[...] = x_ref[...] + jnp.dot(o_in_ref[...].astype(BF16), w_ref[...],
                                      preferred_element_type=F32)


def _ab_out(x, o_a, y, wglu_bf, w_bf):
    n = x.shape[0]
    tm = min(512, n)
    return pl.pallas_call(
        _ab_out_kernel,
        grid=(n // tm,),
        in_specs=[_row_spec(tm, D_MODEL), _row_spec(tm, 512), _row_spec(tm, 512),
                  _const_spec((512, 512)), _const_spec((D_MODEL, D_MODEL))],
        out_specs=_row_spec(tm, D_MODEL),
        out_shape=jax.ShapeDtypeStruct((n, D_MODEL), F32),
        compiler_params=_cparams(("arbitrary",)),
        name="ab_out",
    )(x, o_a, y, wglu_bf, w_bf)


def _c_out(x, o, w_bf):
    n = x.shape[0]
    tm = min(512, n)
    return pl.pallas_call(
        _c_out_kernel,
        grid=(n // tm,),
        in_specs=[_row_spec(tm, D_MODEL), _row_spec(tm, D_MODEL), _const_spec((D_MODEL, D_MODEL))],
        out_specs=_row_spec(tm, D_MODEL),
        out_shape=jax.ShapeDtypeStruct((n, D_MODEL), F32),
        compiler_params=_cparams(("arbitrary",)),
        name="c_out",
    )(x, o, w_bf)


FF_SUB_BLOCKS = 4


def _ffn_kernel(*refs, tm, carry, period):
    if carry:
        x_ref, g_ref, wg_ref, wu_ref, cw_ref, wdn_ref, y_ref, cs_ref, hb_ref, acc_ref, halo_ref = refs
    else:
        (x_ref, g_ref, wg_ref, wu_ref, cw_ref, wdn_ref, old0_ref, old1_ref, y_ref, gout_ref,
         hb_ref, acc_ref) = refs
    i = pl.program_id(1)
    c = pl.program_id(2)

    @pl.when(c == 0)
    def _():
        x = x_ref[0]
        hb_ref[...] = _rms_rows(x, g_ref[...]).astype(BF16)
        acc_ref[...] = x

    sub = tm // FF_SUB_BLOCKS
    assert sub % 8 == 0 and (carry or sub % period == 0)
    blocks = [slice(s * sub, (s + 1) * sub) for s in range(FF_SUB_BLOCKS)]
    gates = [jnp.dot(hb_ref[r, :], wg_ref[...], preferred_element_type=F32) for r in blocks]
    ups = [jnp.dot(hb_ref[r, :], wu_ref[...], preferred_element_type=F32) for r in blocks]
    cw = cw_ref[...]
    row = lax.broadcasted_iota(jnp.int32, (sub, FF_CHUNK), 0)
    if carry:
        prev = jnp.where(i == 0, 0.0, halo_ref[c])
    for s, r in enumerate(blocks):
        gate = gates[s]
        r1 = pltpu.roll(gate, 1, 0)
        r2 = pltpu.roll(gate, 2, 0)
        if carry:
            p1 = jnp.where(row == 0, prev[7:8], r1)
            p2 = jnp.where(row == 0, prev[6:7], jnp.where(row == 1, prev[7:8], r2))
            prev = gate[sub - 8:sub]
        else:
            t = row & (period - 1)
            nseq = sub // period
            rep = lambda b: jnp.broadcast_to(b[:, None, :], (nseq, period, FF_CHUNK)).reshape(sub, FF_CHUNK)
            old0 = rep(old0_ref[s * nseq:(s + 1) * nseq, :])
            old1 = rep(old1_ref[s * nseq:(s + 1) * nseq, :])
            p1 = jnp.where(t == 0, old1, r1)
            p2 = jnp.where(t == 0, old0, jnp.where(t == 1, old1, r2))
            gout_ref[0, r, :] = gate
        conv = cw[3:4] + cw[0:1] * p2 + cw[1:2] * p1 + cw[2:3] * gate
        act = (conv * _sigmoid(conv)) * ups[s]
        acc_ref[r, :] += jnp.dot(act.astype(BF16), wdn_ref[...], preferred_element_type=F32)
    if carry:
        halo_ref[c] = prev
        cs_ref[0, 0] = prev

    @pl.when(c == N_FF_CHUNKS - 1)
    def _():
        y_ref[0] = acc_ref[...]


def _ffn_weights(gain, w_up, conv_w, conv_b, w_down):
    cw = jnp.concatenate([conv_w, conv_b[None], jnp.zeros((4, D_FF), F32)], axis=0)
    return gain.reshape(1, D_MODEL), w_up.astype(BF16), cw, w_down.astype(BF16)


def _ffn_common_specs():
    return [pl.BlockSpec((1, D_MODEL), lambda b, i, c: (0, 0)),
            pl.BlockSpec((D_MODEL, FF_CHUNK), lambda b, i, c: (0, c)),
            pl.BlockSpec((D_MODEL, FF_CHUNK), lambda b, i, c: (0, N_FF_CHUNKS + c)),
            pl.BlockSpec((8, FF_CHUNK), lambda b, i, c: (0, c)),
            pl.BlockSpec((FF_CHUNK, D_MODEL), lambda b, i, c: (c, 0))]


def _ffn_prompt(x, weights, *, batch, seq):
    tm = min(1024, seq)
    x_spec = pl.BlockSpec((1, tm, D_MODEL), lambda b, i, c: (b, i, 0))
    y, cs = pl.pallas_call(
        functools.partial(_ffn_kernel, tm=tm, carry=True, period=0),
        grid=(batch, seq // tm, N_FF_CHUNKS),
        in_specs=[x_spec] + _ffn_common_specs(),
        out_specs=(x_spec, pl.BlockSpec((1, 1, 8, FF_CHUNK), lambda b, i, c: (b * (seq // tm) + i, c, 0, 0))),
        out_shape=(jax.ShapeDtypeStruct((batch, seq, D_MODEL), F32),
                   jax.ShapeDtypeStruct((batch * (seq // tm), N_FF_CHUNKS, 8, FF_CHUNK), F32)),
        scratch_shapes=[pltpu.VMEM((tm, D_MODEL), BF16), pltpu.VMEM((tm, D_MODEL), F32),
                        pltpu.VMEM((N_FF_CHUNKS, 8, FF_CHUNK), F32)],
        compiler_params=_cparams(("arbitrary", "arbitrary", "arbitrary")),
        name="ffn_prompt",
    )(x, weights[0], weights[1], *weights[1:])
    cs = cs.reshape(batch, seq // tm, N_FF_CHUNKS, 8, FF_CHUNK)[:, -1]
    conv_state = cs[:, :, 6:8, :].transpose(0, 2, 1, 3).reshape(batch, 2, D_FF)
    return y, conv_state


def _ffn_sample(x, conv_buf, weights, *, batch, seq):
    n = batch * seq
    x_spec = pl.BlockSpec((1, n, D_MODEL), lambda b, i, c: (0, 0, 0))
    col_spec = pl.BlockSpec((1, n, FF_CHUNK), lambda b, i, c: (0, 0, c))
    buf_spec = pl.BlockSpec((batch, FF_CHUNK), lambda b, i, c: (0, c))
    y, gate = pl.pallas_call(
        functools.partial(_ffn_kernel, tm=n, carry=False, period=seq),
        grid=(1, 1, N_FF_CHUNKS),
        in_specs=[x_spec] + _ffn_common_specs() + [buf_spec, buf_spec],
        out_specs=(x_spec, col_spec),
        out_shape=(jax.ShapeDtypeStruct((1, n, D_MODEL), F32), jax.ShapeDtypeStruct((1, n, D_FF), F32)),
        scratch_shapes=[pltpu.VMEM((n, D_MODEL), BF16), pltpu.VMEM((n, D_MODEL), F32)],
        compiler_params=_cparams(("arbitrary", "arbitrary", "arbitrary")),
        name="ffn_sample",
    )(x[None], weights[0], weights[1], *weights[1:], conv_buf[:, 0], conv_buf[:, 1])
    return y[0], gate.reshape(batch, seq, D_FF)[:, seq - 2:]


def kernel(x_prompt, x_sample, cache_moba_k, cache_moba_v, state_s5_re, state_s5_im, cache_diff_k, cache_diff_v, state_ffn_conv, page_table, rel_bias, ab_norm, w_ab_in, w_ab_out, moba_q_gain, moba_k_gain, s5_a_re, s5_a_im, s5_log_dt, s5_b_re, s5_b_im, s5_c_re, s5_c_im, s5_d, s5_w_glu, c_norm, w_c_in, w_c_out, diff_q_gain, diff_k_gain, diff_lq1, diff_lk1, diff_lq2, diff_lk2, diff_head_gain, ffn_norm, w_ffn_up, ffn_conv_w, ffn_conv_b, w_ffn_down):
    batch, seq, _ = x_prompt.shape
    dbatch, dseq, _ = x_sample.shape
    depth = ffn_norm.shape[0]
    n_pages = page_table.shape[1]
    past_len = n_pages * PAGE_SIZE
    n_phys = cache_moba_k.shape[0]

    bias_tiles, bias_rows = _build_bias(rel_bias, past_len, dseq)
    gmat = _group_matrix()
    moba_k_pages = cache_moba_k.transpose(0, 1, 3, 4, 2).reshape(n_phys, -1, MOBA_WIDTH, PAGE_SIZE)
    moba_v_pages = cache_moba_v.transpose(0, 1, 3, 4, 2).reshape(n_phys, -1, MOBA_WIDTH, PAGE_SIZE)
    diff_k_pages = cache_diff_k.transpose(0, 1, 3, 4, 5, 2).reshape(n_phys, -1, D_MODEL, PAGE_SIZE)
    diff_v_pages = cache_diff_v.reshape(n_phys, -1, PAGE_SIZE * DIFF_HEADS, 2 * HEAD_DIM)
    diff_bias_rows = jnp.broadcast_to(
        bias_rows.reshape(DIFF_HEADS, 1, dseq, -1), (DIFF_HEADS, 2, dseq, bias_rows.shape[-1])
    ).reshape(2 * DIFF_HEADS * dseq, -1)
    zero_state = jnp.zeros((batch, S5_GROUPS, S5_STATE), F32)
    n_ab, n_c = (depth + 1) // 2, depth // 2
    mk_all = jnp.zeros((batch, n_ab, MOBA_WIDTH, seq), F32)
    mv_all = jnp.zeros((batch, n_ab, MOBA_WIDTH, seq), F32)
    dk_all = jnp.zeros((batch, n_c, D_MODEL, seq), F32)
    dv_all = jnp.zeros((batch, n_c, seq, D_MODEL), F32)

    xp = x_prompt.reshape(batch * seq, D_MODEL)
    xs = x_sample.reshape(dbatch * dseq, D_MODEL)
    mk_s, mv_s = [], []
    sr_p, sr_s, si_p, si_s = [], [], [], []
    dk_s, dv_s = [], []
    cb_p, cb_s = [], []
    row = lambda v: v.reshape(1, -1)
    for layer in range(depth):
        li = layer // 2
        if layer % 2 == 0:
            gain = row(ab_norm[li])
            w_in = w_ab_in[li].astype(BF16)
            w_out = w_ab_out[li].astype(BF16)
            w_glu = s5_w_glu[li].astype(BF16)
            q_gain = row(jnp.tile(moba_q_gain[li], MOBA_WIDTH // HEAD_DIM))
            k_gain = row(jnp.tile(moba_k_gain[li], MOBA_WIDTH // HEAD_DIM))
            s5_args = (s5_a_re[li], s5_a_im[li], s5_log_dt[li], s5_b_re[li], s5_b_im[li],
                       s5_c_re[li], s5_c_im[li], s5_d[li])
            qa, qb, kb, mk_all, mv_all, vt, u, kmean = _in_proj(
                xp, gain, w_in, q_gain, k_gain, gmat, width=MOBA_WIDTH, has_u=True, seq=seq,
                rows_all=(mk_all, mv_all), li=li)
            shp = lambda a: a.reshape(batch, seq, MOBA_WIDTH)
            o_a = _attn_prompt("moba", shp(qa), shp(qb), shp(kb), vt, bias_tiles,
                               (kmean.reshape(batch, seq // MOBA_BLOCK, MOBA_WIDTH),), batch=batch)
            tabs = _s5_tables(*s5_args, chunk=16)
            y, s_re, s_im = _s5(u, tabs, zero_state, zero_state, batch=batch, seq=seq)
            xp = _ab_out(xp, o_a.reshape(batch * seq, MOBA_WIDTH), y, w_glu, w_out)
            sr_p.append(s_re); si_p.append(s_im)
            q, k, v, u = _in_proj(xs, gain, w_in, q_gain, k_gain, gmat, width=MOBA_WIDTH, has_u=True)
            shs = lambda a: a.reshape(dbatch, dseq, MOBA_WIDTH)
            o_a = _sample_attn("moba", page_table, shs(q), shs(k), shs(v), bias_rows,
                               moba_k_pages, moba_v_pages, li)
            tabs = _s5_tables(*s5_args, chunk=dseq)
            y, s_re, s_im = _s5(u, tabs, state_s5_re[li], state_s5_im[li], batch=dbatch, seq=dseq)
            xs = _ab_out(xs, o_a.reshape(dbatch * dseq, MOBA_WIDTH), y, w_glu, w_out)
            mk_s.append(k.reshape(dbatch, dseq, 8, HEAD_DIM)); mv_s.append(v.reshape(dbatch, dseq, 8, HEAD_DIM))
            sr_s.append(s_re); si_s.append(s_im)
        else:
            lam_init = 0.8 - 0.6 * math.exp(-0.3 * layer)
            gain = row(c_norm[li])
            w_in = w_c_in[li].astype(BF16)
            w_out = w_c_out[li].astype(BF16)
            q_gain = row(jnp.tile(diff_q_gain[li], D_MODEL // HEAD_DIM))
            k_gain = row(jnp.tile(diff_k_gain[li], D_MODEL // HEAD_DIM))
            lams = (row(diff_lq1[li]), row(diff_lk1[li]), row(diff_lq2[li]), row(diff_lk2[li]))
            head_gain = row(diff_head_gain[li])
            qa, qb, kb, dk_all, dv_all, vt = _in_proj(
                xp, gain, w_in, q_gain, k_gain, gmat, width=D_MODEL, has_u=False, seq=seq,
                rows_all=(dk_all, dv_all), li=li)
            shp = lambda a: a.reshape(batch, seq, D_MODEL)
            o = _attn_prompt("diff", shp(qa), shp(qb), shp(kb), vt, bias_tiles,
                             lams + (head_gain,), batch=batch, lam_init=lam_init)
            xp = _c_out(xp, o.reshape(batch * seq, D_MODEL), w_out)
            q, k, v = _in_proj(xs, gain, w_in, q_gain, k_gain, gmat, width=D_MODEL, has_u=False)
            shs = lambda a: a.reshape(dbatch, dseq, D_MODEL)
            o = _sample_attn("diff", page_table, shs(q), shs(k), shs(v), diff_bias_rows,
                             diff_k_pages, diff_v_pages, li, extra=lams + (head_gain,),
                             lam_init=lam_init)
            xs = _c_out(xs, o.reshape(dbatch * dseq, D_MODEL), w_out)
            dk_s.append(k.reshape(dbatch, dseq, 8, 2, HEAD_DIM)); dv_s.append(v.reshape(dbatch, dseq, 8, 2 * HEAD_DIM))
        weights = _ffn_weights(ffn_norm[layer], w_ffn_up[layer], ffn_conv_w[layer], ffn_conv_b[layer],
                               w_ffn_down[layer])
        yp, buf_p = _ffn_prompt(xp.reshape(batch, seq, D_MODEL), weights, batch=batch, seq=seq)
        xp = yp.reshape(batch * seq, D_MODEL)
        xs, buf_s = _ffn_sample(xs, state_ffn_conv[layer], weights, batch=dbatch, seq=dseq)
        cb_p.append(buf_p); cb_s.append(buf_s)

    st = lambda xs_, ax: jnp.stack(xs_, axis=ax)
    mk_p = mk_all.reshape(batch, n_ab, 8, HEAD_DIM, seq).transpose(0, 1, 4, 2, 3)
    mv_p = mv_all.reshape(batch, n_ab, 8, HEAD_DIM, seq).transpose(0, 1, 4, 2, 3)
    dk_p = dk_all.reshape(batch, n_c, 8, 2, HEAD_DIM, seq).transpose(0, 1, 5, 2, 3, 4)
    dv_p = dv_all.reshape(batch, n_c, seq, 8, 2 * HEAD_DIM)
    return (xp.reshape(batch, seq, D_MODEL), xs.reshape(dbatch, dseq, D_MODEL),
            mk_p, st(mk_s, 1), mv_p, st(mv_s, 1),
            st(sr_p, 0), st(sr_s, 0), st(si_p, 0), st(si_s, 0),
            dk_p, st(dk_s, 1), dv_p, st(dv_s, 1),
            st(cb_p, 0), st(cb_s, 0))
```
